```python
import math
import jax
import jax.numpy as jnp
from jax import lax
import numpy as np

D_MODEL = 1024
BATCH = 2
SEQ = 8192
DEPTH = 4

GRID_W = 64
CTX_LEN = 256

SSM_WIDTH = 512
SSM_GROUP = 16
SSM_GROUPS = SSM_WIDTH // SSM_GROUP
SSM_STATE = 64
SSM_DT_MIN = 1e-3
SSM_DT_MAX = 1e-1

NA_HEADS = 8
NA_HEAD_DIM = 64
NA_WIDTH = NA_HEADS * NA_HEAD_DIM
NA_WIN_ROWS = 8
NA_WIN_COLS = 16

MLA_HEADS = 8
MLA_Q_RANK = 256
MLA_KV_RANK = 128
MLA_NOPE_DIM = 64
MLA_ROPE_DIM = 32
MLA_V_DIM = 64
MLA_QK_DIM = MLA_NOPE_DIM + MLA_ROPE_DIM
MLA_WIDTH = MLA_HEADS * MLA_V_DIM

Q_BLOCK = 128
ROPE_THETA = 10000.0
LN_EPS = 1e-5
RMS_EPS = 1e-6
NEG_INF = -1e30
DEEPNORM_ALPHA = (2 * DEPTH) ** 0.25
DEEPNORM_BETA = (8 * DEPTH) ** -0.25
NA_SCALE = NA_HEAD_DIM ** -0.5
MLA_SCALE = MLA_QK_DIM ** -0.5
F32 = jnp.float32

IN_SPLITS = (SSM_WIDTH, SSM_WIDTH,
             NA_WIDTH, NA_WIDTH, NA_WIDTH, NA_WIDTH,
             MLA_Q_RANK, MLA_KV_RANK, MLA_ROPE_DIM, MLA_WIDTH,
             D_MODEL, D_MODEL, D_MODEL)
IN_WIDTH = sum(IN_SPLITS)

kernel_name = 'hybrid_s5_natten_mla_prefix_block'


def layer_norm(x, g=None, b=None):
    xf = x.astype(F32)
    xc = xf - jnp.mean(xf, axis=-1, keepdims=True)
    y = xc * lax.rsqrt(jnp.mean(xc * xc, axis=-1, keepdims=True) + LN_EPS)
    if g is not None:
        y = y * g.astype(F32) + b.astype(F32)
    return y.astype(x.dtype)


def rms_norm(x, g):
    xf = x.astype(F32)
    y = xf * lax.rsqrt(jnp.mean(xf * xf, axis=-1, keepdims=True) + RMS_EPS) * g.astype(F32)
    return y.astype(x.dtype)


def split_cols(p):
    idx = [int(i) for i in np.cumsum(IN_SPLITS)[:-1]]
    return jnp.split(p, idx, axis=-1)


def axial_rope_tables(n_tokens):
    t = jnp.arange(n_tokens, dtype=jnp.int32)
    row = (t // GRID_W).astype(F32)
    col = (t % GRID_W).astype(F32)
    half = MLA_ROPE_DIM // 2
    inv = 1.0 / (ROPE_THETA ** (jnp.arange(0, half, 2, dtype=F32) / half))
    ang_r = row[:, None] * inv[None, :]
    ang_c = col[:, None] * inv[None, :]
    return (jnp.cos(ang_r), jnp.sin(ang_r), jnp.cos(ang_c), jnp.sin(ang_c))


def _rotate(x, cos, sin):
    n = x.shape[-1] // 2
    x1 = x[..., :n].astype(F32)
    x2 = x[..., n:].astype(F32)
    cos = cos[None, :, None, :]
    sin = sin[None, :, None, :]
    return jnp.concatenate([x1 * cos - x2 * sin, x1 * sin + x2 * cos], axis=-1).astype(x.dtype)


def apply_axial_rope(x, rope):
    cos_r, sin_r, cos_c, sin_c = rope
    half = x.shape[-1] // 2
    return jnp.concatenate([_rotate(x[..., :half], cos_r, sin_r),
                            _rotate(x[..., half:], cos_c, sin_c)], axis=-1)


def block_attention(q, k, v, scale):
    B, S, H, dq = q.shape
    nb = S // Q_BLOCK
    qb = jnp.moveaxis(q.reshape(B, nb, Q_BLOCK, H, dq), 1, 0)

    def attend(q_blk):
        s = jnp.einsum('bqhd,bkhd->bhqk', q_blk, k).astype(F32) * scale
        p = jax.nn.softmax(s, axis=-1).astype(v.dtype)
        return jnp.einsum('bhqk,bkhd->bqhd', p, v)

    o = lax.map(attend, qb)
    return jnp.moveaxis(o, 0, 1).reshape(B, S, H * v.shape[-1])


def _complex_combine(e1, e2):
    a1r, a1i, b1r, b1i = e1
    a2r, a2i, b2r, b2i = e2
    return (a2r * a1r - a2i * a1i, a2r * a1i + a2i * a1r,
            a2r * b1r - a2i * b1i + b2r, a2r * b1i + a2i * b1r + b2i)


def s5_discretise(lam_re, lam_im, log_dt, b_re, b_im):
    dt = jnp.exp(log_dt.astype(F32))[:, None]
    lr = lam_re.astype(F32)
    li = lam_im.astype(F32)
    mag = jnp.exp(lr * dt)
    ab_re = mag * jnp.cos(li * dt)
    ab_im = mag * jnp.sin(li * dt)
    nr = ab_re - 1.0
    den = lr * lr + li * li
    fr = (nr * lr + ab_im * li) / den
    fi = (ab_im * lr - nr * li) / den
    br = b_re.astype(F32)
    bi = b_im.astype(F32)
    bb_re = fr[..., None] * br - fi[..., None] * bi
    bb_im = fr[..., None] * bi + fi[..., None] * br
    return ab_re, ab_im, bb_re, bb_im


def s5_states(u, disc, s0):
    ab_re, ab_im, bb_re, bb_im = disc
    bu_re = jnp.einsum('blgi,gpi->blgp', u, bb_re)
    bu_im = jnp.einsum('blgi,gpi->blgp', u, bb_im)
    if s0 is not None:
        s0_re, s0_im = s0
        bu_re = bu_re.at[:, 0].add(ab_re * s0_re - ab_im * s0_im)
        bu_im = bu_im.at[:, 0].add(ab_re * s0_im + ab_im * s0_re)
    a_re = jnp.broadcast_to(ab_re, bu_re.shape)
    a_im = jnp.broadcast_to(ab_im, bu_im.shape)
    _, _, s_re, s_im = lax.associative_scan(_complex_combine, (a_re, a_im, bu_re, bu_im), axis=1)
    return s_re, s_im


def s5_readout(states, c_re, c_im):
    s_re, s_im = states
    return jnp.einsum('gip,blgp->blgi', c_re, s_re) - jnp.einsum('gip,blgp->blgi', c_im, s_im)


def s5_output(y, u, d_skip, w_glu, b_glu):
    B, L, W = u.shape
    y = (y.reshape(B, L, W) + d_skip.astype(F32) * u.astype(F32)).astype(u.dtype)
    y = jax.nn.gelu(y)
    return y * jax.nn.sigmoid(y @ w_glu + b_glu)


def s5_mixer(u_lat, u_ctx, lam_re, lam_im, log_dt, b_re, b_im, c_re, c_im,
             d_skip, w_glu, b_glu, need_ctx):
    B, S, _ = u_lat.shape
    ul = u_lat.astype(F32).reshape(B, S, SSM_GROUPS, SSM_GROUP)
    uc = u_ctx.astype(F32).reshape(B, u_ctx.shape[1], SSM_GROUPS, SSM_GROUP)
    y_lat = 0.0
    y_ctx = 0.0
    for d in range(2):
        disc = s5_discretise(lam_re[d], lam_im[d], log_dt[d], b_re[d], b_im[d])
        cr = c_re[d].astype(F32)
        ci = c_im[d].astype(F32)
        ucd, uld = (uc, ul) if d == 0 else (uc[:, ::-1], ul[:, ::-1])
        sc = s5_states(ucd, disc, None)
        sl = s5_states(uld, disc, (sc[0][:, -1], sc[1][:, -1]))
        yl = s5_readout(sl, cr, ci)
        y_lat = y_lat + (yl if d == 0 else yl[:, ::-1])
        if need_ctx:
            yc = s5_readout(sc, cr, ci)
            y_ctx = y_ctx + (yc if d == 0 else yc[:, ::-1])
    out_lat = s5_output(y_lat, u_lat, d_skip, w_glu, b_glu)
    out_ctx = s5_output(y_ctx, u_ctx, d_skip, w_glu, b_glu) if need_ctx else None
    return out_lat, out_ctx


def neighbourhood_attention(q, k, v, k_ctx, v_ctx, rpb):
    B, S, H, dh = q.shape
    rows = S // GRID_W
    wr = min(NA_WIN_ROWS, rows)
    wc = NA_WIN_COLS
    r = jnp.arange(rows)
    key_rows = jnp.clip(r - wr // 2, 0, rows - wr)[:, None] + jnp.arange(wr)[None, :]
    n_lat = wr * GRID_W
    qr = q.reshape(B, rows, GRID_W, H, dh)
    kg = k.reshape(B, rows, GRID_W, H, dh)[:, key_rows].reshape(B, rows, n_lat, H, dh)
    vg = v.reshape(B, rows, GRID_W, H, dh)[:, key_rows].reshape(B, rows, n_lat, H, dh)
    col = jnp.arange(GRID_W)
    col_start = jnp.clip(col - wc // 2, 0, GRID_W - wc)
    in_win = (col[None, :] >= col_start[:, None]) & (col[None, :] < col_start[:, None] + wc)
    mask = jnp.broadcast_to(in_win[:, None, :], (GRID_W, wr, GRID_W)).reshape(GRID_W, n_lat)
    dr = key_rows - r[:, None]
    dc = col[None, :] - col[:, None]
    idx_r = (dr + NA_WIN_ROWS - 1)[:, None, :, None]
    idx_c = (jnp.clip(dc, -(wc - 1), wc - 1) + NA_WIN_COLS - 1)[None, :, None, :]
    bias = rpb[:, idx_r, idx_c].reshape(H, rows, GRID_W, n_lat).astype(F32)
    s_lat = jnp.einsum('brqhd,brkhd->bhrqk', qr, kg).astype(F32) * NA_SCALE + bias[None]
    s_lat = jnp.where(mask, s_lat, NEG_INF)
    s_ctx = jnp.einsum('brqhd,bkhd->bhrqk', qr, k_ctx).astype(F32) * NA_SCALE
    p = jax.nn.softmax(jnp.concatenate([s_lat, s_ctx], axis=-1), axis=-1).astype(v.dtype)
    o = (jnp.einsum('bhrqk,brkhd->brqhd', p[..., :n_lat], vg)
         + jnp.einsum('bhrqk,bkhd->brqhd', p[..., n_lat:], v_ctx))
    return o.reshape(B, S, H * dh)


def mla_queries(c_q, q_norm, w_uq, rope):
    B, L, _ = c_q.shape
    q = (rms_norm(c_q, q_norm) @ w_uq).reshape(B, L, MLA_HEADS, MLA_QK_DIM)
    q_nope = q[..., :MLA_NOPE_DIM]
    q_pe = q[..., MLA_NOPE_DIM:]
    if rope is not None:
        q_pe = apply_axial_rope(q_pe, rope)
    return jnp.concatenate([q_nope, q_pe], axis=-1)


def mla_keys_values(c_kv, k_rope, kv_norm, w_ukv, rope):
    B, L, _ = c_kv.shape
    kv = (rms_norm(c_kv, kv_norm) @ w_ukv).reshape(B, L, MLA_HEADS, MLA_NOPE_DIM + MLA_V_DIM)
    k_nope = kv[..., :MLA_NOPE_DIM]
    v = kv[..., MLA_NOPE_DIM:]
    k_pe = k_rope[:, :, None, :]
    if rope is not None:
        k_pe = apply_axial_rope(k_pe, rope)
    k = jnp.concatenate([k_nope, jnp.broadcast_to(k_pe, (B, L, MLA_HEADS, MLA_ROPE_DIM))], axis=-1)
    return k, v


def merge_branches(ya, za, yn, zn, ym, zm, ga, gn, gm, w_branch_a, w_branch_b, w_branch_c, w_out):
    silu = jax.nn.silu
    sig = jax.nn.sigmoid
    m = (sig(ga) * ((ya * silu(za)) @ w_branch_a)
         + sig(gn) * ((yn * silu(zn)) @ w_branch_b)
         + sig(gm) * ((ym * silu(zm)) @ w_branch_c))
    return m @ w_out


def na_heads(t):
    return t.reshape(t.shape[0], t.shape[1], NA_HEADS, NA_HEAD_DIM)


def trunk_layer(x, ctx, mod_lat, mod_ctx, w_in, lam_re, lam_im, log_dt, b_re, b_im, c_re, c_im,
                d_skip, w_glu, b_glu, rpb, q_norm, w_uq, kv_norm, w_ukv,
                w_branch_a, w_branch_b, w_branch_c, w_out, ln_g, ln_b, rope, need_ctx):
    shift, scale, gate = jnp.split(mod_lat, 3, axis=-1)
    shift_c, scale_c, gate_c = jnp.split(mod_ctx, 3, axis=-1)
    h = layer_norm(x) * (1.0 + scale[:, None]) + shift[:, None]
    hc = layer_norm(ctx) * (1.0 + scale_c) + shift_c
    (ua, za, qn, kn, vn, zn, cq, ckv, kr, zm, ga, gn, gm) = split_cols(h @ w_in)
    (ua_c, za_c, qn_c, kn_c, vn_c, zn_c, cq_c, ckv_c, kr_c, zm_c, ga_c, gn_c, gm_c) = split_cols(hc @ w_in)

    ya, ya_c = s5_mixer(ua, ua_c, lam_re, lam_im, log_dt, b_re, b_im, c_re, c_im,
                        d_skip, w_glu, b_glu, need_ctx)
    kn_ch = na_heads(kn_c)
    vn_ch = na_heads(vn_c)
    yn = neighbourhood_attention(na_heads(qn), na_heads(kn), na_heads(vn), kn_ch, vn_ch, rpb)
    q_l = mla_queries(cq, q_norm, w_uq, rope)
    k_l, v_l = mla_keys_values(ckv, kr, kv_norm, w_ukv, rope)
    k_c, v_c = mla_keys_values(ckv_c, kr_c, kv_norm, w_ukv, None)
    ym = block_attention(q_l, jnp.concatenate([k_l, k_c], axis=1),
                         jnp.concatenate([v_l, v_c], axis=1), MLA_SCALE)

    out = merge_branches(ya, za, yn, zn, ym, zm, ga, gn, gm, w_branch_a, w_branch_b, w_branch_c, w_out)
    x_new = layer_norm(DEEPNORM_ALPHA * x + gate[:, None] * out, ln_g, ln_b)
    if not need_ctx:
        return x_new, None

    yn_c = block_attention(na_heads(qn_c), kn_ch, vn_ch, NA_SCALE)
    q_c = mla_queries(cq_c, q_norm, w_uq, None)
    ym_c = block_attention(q_c, k_c, v_c, MLA_SCALE)
    out_c = merge_branches(ya_c, za_c, yn_c, zn_c, ym_c, zm_c, ga_c, gn_c, gm_c,
                           w_branch_a, w_branch_b, w_branch_c, w_out)
    ctx_new = layer_norm(DEEPNORM_ALPHA * ctx + gate_c * out_c, ln_g, ln_b)
    return x_new, ctx_new


def setup_inputs(seed: int = 0) -> dict:
    key = jax.random.key(seed)
    ks = jax.random.split(key, 32)
    D = D_MODEL
    G = SSM_GROUPS
    P = SSM_STATE
    Gi = SSM_GROUP

    def nrm(k, shape, s):
        return jax.random.normal(k, shape, F32) * s

    return {
        'x': nrm(ks[0], (BATCH, SEQ, D), 1.0),
        'c': nrm(ks[1], (BATCH, D), 1.0),
        'ctx': nrm(ks[2], (BATCH, CTX_LEN, D), 1.0),
        'c_ctx': nrm(ks[3], (D,), 1.0),
        'w_mod': nrm(ks[4], (DEPTH, D, 3 * D), D ** -0.5),
        'b_mod': nrm(ks[5], (DEPTH, 3 * D), 0.02),
        'w_in': nrm(ks[6], (DEPTH, D, IN_WIDTH), D ** -0.5),
        'ssm_lam_re': -0.5 + nrm(ks[7], (DEPTH, 2, G, P), 0.01),
        'ssm_lam_im': math.pi * jnp.arange(P, dtype=F32) + nrm(ks[8], (DEPTH, 2, G, P), 0.01),
        'ssm_log_dt': jax.random.uniform(ks[9], (DEPTH, 2, G), F32,
                                         math.log(SSM_DT_MIN), math.log(SSM_DT_MAX)),
        'ssm_b_re': nrm(ks[10], (DEPTH, 2, G, P, Gi), (2 * Gi) ** -0.5),
        'ssm_b_im': nrm(ks[11], (DEPTH, 2, G, P, Gi), (2 * Gi) ** -0.5),
        'ssm_c_re': nrm(ks[12], (DEPTH, 2, G, Gi, P), 0.5),
        'ssm_c_im': nrm(ks[13], (DEPTH, 2, G, Gi, P), 0.5),
        'ssm_d': nrm(ks[14], (DEPTH, SSM_WIDTH), 1.0),
        'ssm_w_glu': nrm(ks[15], (DEPTH, SSM_WIDTH, SSM_WIDTH), SSM_WIDTH ** -0.5),
        'ssm_b_glu': nrm(ks[16], (DEPTH, SSM_WIDTH), 0.02),
        'na_rpb': nrm(ks[17], (DEPTH, NA_HEADS, 2 * NA_WIN_ROWS - 1, 2 * NA_WIN_COLS - 1), 0.02),
        'mla_q_norm': 1.0 + nrm(ks[18], (DEPTH, MLA_Q_RANK), 0.02),
        'mla_w_uq': nrm(ks[19], (DEPTH, MLA_Q_RANK, MLA_HEADS * MLA_QK_DIM), MLA_Q_RANK ** -0.5),
        'mla_kv_norm': 1.0 + nrm(ks[20], (DEPTH, MLA_KV_RANK), 0.02),
        'mla_w_ukv': nrm(ks[21], (DEPTH, MLA_KV_RANK, MLA_HEADS * (MLA_NOPE_DIM + MLA_V_DIM)),
                         MLA_KV_RANK ** -0.5),
        'w_branch_a': nrm(ks[22], (DEPTH, SSM_WIDTH, D), SSM_WIDTH ** -0.5),
        'w_branch_b': nrm(ks[23], (DEPTH, NA_WIDTH, D), NA_WIDTH ** -0.5),
        'w_branch_c': nrm(ks[24], (DEPTH, MLA_WIDTH, D), MLA_WIDTH ** -0.5),
        'w_out': nrm(ks[25], (DEPTH, D, D), DEEPNORM_BETA * D ** -0.5),
        'ln_g': 1.0 + nrm(ks[26], (DEPTH, D), 0.02),
        'ln_b': nrm(ks[27], (DEPTH, D), 0.02),
    }


def reference(x, c, ctx, c_ctx, w_mod, b_mod, w_in, ssm_lam_re, ssm_lam_im, ssm_log_dt,
              ssm_b_re, ssm_b_im, ssm_c_re, ssm_c_im, ssm_d, ssm_w_glu, ssm_b_glu, na_rpb,
              mla_q_norm, mla_w_uq, mla_kv_norm, mla_w_ukv, w_branch_a, w_branch_b, w_branch_c,
              w_out, ln_g, ln_b):
    rope = axial_rope_tables(x.shape[1])
    sc = jax.nn.silu(c)
    scc = jax.nn.silu(c_ctx)
    for l in range(DEPTH):
        need_ctx = l < DEPTH - 1
        mod_lat = sc @ w_mod[l] + b_mod[l]
        mod_ctx = scc @ w_mod[l] + b_mod[l]
        x, ctx = trunk_layer(x, ctx, mod_lat, mod_ctx, w_in[l],
                             ssm_lam_re[l], ssm_lam_im[l], ssm_log_dt[l],
                             ssm_b_re[l], ssm_b_im[l], ssm_c_re[l], ssm_c_im[l],
                             ssm_d[l], ssm_w_glu[l], ssm_b_glu[l], na_rpb[l],
                             mla_q_norm[l], mla_w_uq[l], mla_kv_norm[l], mla_w_ukv[l],
                             w_branch_a[l], w_branch_b[l], w_branch_c[l], w_out[l],
                             ln_g[l], ln_b[l], rope, need_ctx)
    return x
```

```python
import functools
import math

import jax
import jax.numpy as jnp
import numpy as np
from jax import lax
from jax.experimental import pallas as pl
from jax.experimental.pallas import tpu as pltpu

F32 = jnp.float32
BF16 = jnp.bfloat16

GRID_W = 64
SSM_GROUP = 16
SSM_STATE = 64
SSM_CHUNK = 16
NA_HEADS = 8
NA_HEAD_DIM = 64
NA_WIN_ROWS = 8
NA_WIN_COLS = 16
MLA_HEADS = 8
MLA_Q_RANK = 256
MLA_KV_RANK = 128
MLA_NOPE = 64
MLA_ROPE = 32
MLA_V = 64
MLA_QK = MLA_NOPE + MLA_ROPE
ROPE_THETA = 10000.0
LN_EPS = 1e-5
RMS_EPS = 1e-6
NEG_INF = -1e30
LOG2E = math.log2(math.e)
LANES = 128
HALF = 64

TOKEN_TILE = 256
NA_ROW_BLOCK = 8
FA_TQ = 256
FA_TK = 768
VMEM_LIMIT = 56 * 1024 * 1024


def _sigmoid(x):
    return 1.0 / (1.0 + jnp.exp(-x))


def _dot(a, b):
    return jnp.dot(a, b, preferred_element_type=F32)


def _dot_nt(a, b):
    return lax.dot_general(a, b, (((1,), (1,)), ((), ())), preferred_element_type=F32)


def _params(*sem):
    return pltpu.CompilerParams(dimension_semantics=sem, vmem_limit_bytes=VMEM_LIMIT)


def _mod_kernel(cc_ref, w_ref, b_ref, o_ref):
    cc = cc_ref[...]
    s = (cc * _sigmoid(cc)).astype(BF16)
    o_ref[0] = _dot(s, w_ref[0].astype(BF16)) + b_ref[0]


def _mod_call(cc, w_mod, b_mod):
    depth, d, n = w_mod.shape
    tn = 1024
    return pl.pallas_call(
        _mod_kernel,
        out_shape=jax.ShapeDtypeStruct((depth, 8, n), F32),
        grid=(depth, n // tn),
        in_specs=[pl.BlockSpec((8, d), lambda l, j: (0, 0)),
                  pl.BlockSpec((1, d, tn), lambda l, j: (l, 0, j)),
                  pl.BlockSpec((1, 1, tn), lambda l, j: (l, 0, j))],
        out_specs=pl.BlockSpec((1, 8, tn), lambda l, j: (l, 0, j)),
        compiler_params=_params("parallel", "parallel"),
        name="adaln_mod",
    )(cc, w_mod, b_mod.reshape(depth, 1, n))


IN_COLS = (512, 1536, 1536, 512, 3072)
IN_OFFS = tuple(int(v) for v in np.cumsum((0,) + IN_COLS))


def _in_kernel(x_ref, mod_ref, w_ref, o_u, o_z, o_na, o_mc, o_g):
    x = x_ref[0]
    mu = jnp.mean(x, axis=-1, keepdims=True)
    xc = x - mu
    var = jnp.mean(xc * xc, axis=-1, keepdims=True)
    mod = mod_ref[0, 0]
    h = (xc * lax.rsqrt(var + LN_EPS) * (1.0 + mod[1:2]) + mod[0:1]).astype(BF16)

    def mm(g):
        return _dot(h, w_ref[:, IN_OFFS[g]:IN_OFFS[g + 1]])

    o_u[0] = mm(0).astype(BF16)
    z = mm(1)
    o_z[0] = (z * _sigmoid(z)).astype(BF16)
    o_na[0] = mm(2).astype(BF16)
    o_mc[0] = mm(3).astype(BF16)
    o_g[0] = _sigmoid(mm(4)).astype(BF16)


def _in_call(xcat, modt, w_perm, n_lat_tiles):
    b, ltot, d = xcat.shape
    tm = TOKEN_TILE
    nt = ltot // tm
    tok = lambda w: pl.BlockSpec((1, tm, w), lambda bi, i: (bi, i, 0))
    return pl.pallas_call(
        _in_kernel,
        out_shape=tuple(jax.ShapeDtypeStruct((b, ltot, w), BF16) for w in IN_COLS),
        grid=(b, nt),
        in_specs=[tok(d),
                  pl.BlockSpec((1, 1, 3, d), lambda bi, i: (bi, i // n_lat_tiles, 0, 0)),
                  pl.BlockSpec(w_perm.shape, lambda bi, i: (0, 0), pipeline_mode=pl.Buffered(1))],
        out_specs=tuple(tok(w) for w in IN_COLS),
        compiler_params=_params("parallel", "parallel"),
        name="ln_in_proj",
    )(xcat, modt, w_perm)


def _s5_kernel(u_ref, w_ref, mo_ref, a_ref, y_ref, e_scr, s_scr, *, nb, nlat, nctx):
    u = u_ref[0]
    y_ref[0] = _dot(u, w_ref[0, :, 0:512])
    e_scr[...] = _dot(u, w_ref[0, :, 512:1024])
    ntot = nlat + nctx
    a = a_ref[0]
    arf, aif, arb, aib = a[0:1], a[1:2], a[2:3], a[3:4]

    sub = 8
    ntile, nctx_t, nlat_t = ntot // sub, nctx // sub, nlat // sub

    def sweep(er, ei, sr, si, ar, ai, order):
        rows_r, rows_i = [None] * sub, [None] * sub
        for k in order:
            rows_r[k], rows_i[k] = sr, si
            sr, si = ar * sr - ai * si + er[k:k + 1], ar * si + ai * sr + ei[k:k + 1]
        return jnp.concatenate(rows_r, axis=0), jnp.concatenate(rows_i, axis=0), sr, si

    def body(i, carry):
        tf = jnp.where(i < nctx_t, nlat_t + i, i - nctx_t)
        tb = ntile - 1 - i
        new = []
        for bi in range(nb):
            srf, sif, srb, sib = carry[4 * bi:4 * bi + 4]
            rf = pl.multiple_of(bi * ntot + tf * sub, sub)
            rb = pl.multiple_of(bi * ntot + tb * sub, sub)
            ef = e_scr[pl.ds(rf, sub), 0:256]
            eb = e_scr[pl.ds(rb, sub), 256:512]
            pfr, pfi, srf, sif = sweep(ef[:, 0:128], ef[:, 128:256], srf, sif, arf, aif, range(sub))
            pbr, pbi, srb, sib = sweep(eb[:, 0:128], eb[:, 128:256], srb, sib, arb, aib, range(sub - 1, -1, -1))
            s_scr[pl.ds(rf, sub), 0:128] = pfr
            s_scr[pl.ds(rf, sub), 128:256] = pfi
            s_scr[pl.ds(rb, sub), 256:384] = pbr
            s_scr[pl.ds(rb, sub), 384:512] = pbi
            new += [srf, sif, srb, sib]
        return tuple(new)

    zero = jnp.zeros((1, LANES), F32)
    lax.fori_loop(0, ntile, body, tuple(zero for _ in range(4 * nb)))
    y_ref[0] += _dot(s_scr[...].astype(BF16), mo_ref[0])


def _s5_call(u_pairs, w_pairs, mo_pairs, a_pairs, nb, nlat, nctx):
    npair, r, _ = u_pairs.shape
    kern = functools.partial(_s5_kernel, nb=nb, nlat=nlat, nctx=nctx)
    return pl.pallas_call(
        kern,
        out_shape=jax.ShapeDtypeStruct((npair, r, 512), F32),
        grid=(npair,),
        in_specs=[pl.BlockSpec((1, r, 512), lambda p: (p, 0, 0)),
                  pl.BlockSpec((1, 512, 1024), lambda p: (p, 0, 0)),
                  pl.BlockSpec((1, 512, 512), lambda p: (p, 0, 0)),
                  pl.BlockSpec((1, 4, LANES), lambda p: (p, 0, 0))],
        out_specs=pl.BlockSpec((1, r, 512), lambda p: (p, 0, 0)),
        scratch_shapes=[pltpu.VMEM((r, 512), F32), pltpu.VMEM((r, 512), F32)],
        compiler_params=_params("parallel"),
        name="s5_scan",
    )(u_pairs, w_pairs, mo_pairs, a_pairs)


def _na_kernel(q_ref, k_ref, v_ref, bias_ref, o_ref, *, rb, nrows, nlat, nctx):
    i = pl.program_id(2)
    lane = lax.broadcasted_iota(jnp.int32, (GRID_W, LANES), 1)
    lo = lane < HALF
    hi = lane >= HALF
    kc = k_ref[0, nlat:nlat + nctx, :]
    vc = v_ref[0, nlat:nlat + nctx, :]
    win = NA_WIN_ROWS * GRID_W
    for j in range(rb):
        r = i * rb + j
        start = jnp.clip(r - NA_WIN_ROWS // 2, 0, nrows - NA_WIN_ROWS)
        delta = r - start
        off = pl.multiple_of(start * GRID_W, GRID_W)
        kw = k_ref[0, pl.ds(off, win), :]
        vw = v_ref[0, pl.ds(off, win), :]
        q = q_ref[0, j * GRID_W:(j + 1) * GRID_W, :]
        outs = []
        for hh in range(2):
            qm = jnp.where(lo if hh == 0 else hi, q, jnp.zeros_like(q))
            sl = _dot_nt(qm, kw) + bias_ref[hh, delta]
            sc = _dot_nt(qm, kc)
            m = jnp.maximum(jnp.max(sl, axis=-1, keepdims=True), jnp.max(sc, axis=-1, keepdims=True))
            pw = jnp.exp2(sl - m)
            pc = jnp.exp2(sc - m)
            den = jnp.sum(pw, axis=-1, keepdims=True) + jnp.sum(pc, axis=-1, keepdims=True)
            o = _dot(pw.astype(BF16), vw) + _dot(pc.astype(BF16), vc)
            outs.append(o / den)
        o_ref[0, j * GRID_W:(j + 1) * GRID_W, :] = jnp.where(lo, outs[0], outs[1]).astype(BF16)


def _na_call(na, bias_full, nlat, nctx):
    b, ltot, _ = na.shape
    nrows = nlat // GRID_W
    rb = NA_ROW_BLOCK
    npair = NA_HEADS // 2
    kern = functools.partial(_na_kernel, rb=rb, nrows=nrows, nlat=nlat, nctx=nctx)
    return pl.pallas_call(
        kern,
        out_shape=jax.ShapeDtypeStruct((b, nlat, NA_HEADS * NA_HEAD_DIM), BF16),
        grid=(b, npair, nrows // rb),
        in_specs=[pl.BlockSpec((1, rb * GRID_W, LANES), lambda bi, p, i: (bi, i, p)),
                  pl.BlockSpec((1, ltot, LANES), lambda bi, p, i: (bi, 0, npair + p)),
                  pl.BlockSpec((1, ltot, LANES), lambda bi, p, i: (bi, 0, 2 * npair + p)),
                  pl.BlockSpec((2, NA_WIN_ROWS, GRID_W, NA_WIN_ROWS * GRID_W), lambda bi, p, i: (p, 0, 0, 0))],
        out_specs=pl.BlockSpec((1, rb * GRID_W, LANES), lambda bi, p, i: (bi, i, p)),
        compiler_params=_params("parallel", "parallel", "parallel"),
        name="na_latent",
    )(na, na, na, bias_full)


def _na_ctx_kernel(q_ref, k_ref, v_ref, o_ref):
    n = q_ref.shape[1]
    lane = lax.broadcasted_iota(jnp.int32, (n, LANES), 1)
    lo = lane < HALF
    hi = lane >= HALF
    q = q_ref[0]
    k = k_ref[0]
    v = v_ref[0]
    outs = []
    for hh in range(2):
        qm = jnp.where(lo if hh == 0 else hi, q, jnp.zeros_like(q))
        s = _dot_nt(qm, k)
        p = jnp.exp2(s - jnp.max(s, axis=-1, keepdims=True))
        outs.append(_dot(p.astype(BF16), v) / jnp.sum(p, axis=-1, keepdims=True))
    o_ref[0] = jnp.where(lo, outs[0], outs[1]).astype(BF16)


def _na_ctx_call(na, nlat, nctx):
    b = na.shape[0]
    npair = NA_HEADS // 2
    rblk = nlat // nctx
    return pl.pallas_call(
        _na_ctx_kernel,
        out_shape=jax.ShapeDtypeStruct((b, nctx, NA_HEADS * NA_HEAD_DIM), BF16),
        grid=(b, npair),
        in_specs=[pl.BlockSpec((1, nctx, LANES), lambda bi, p: (bi, rblk, p)),
                  pl.BlockSpec((1, nctx, LANES), lambda bi, p: (bi, rblk, npair + p)),
                  pl.BlockSpec((1, nctx, LANES), lambda bi, p: (bi, rblk, 2 * npair + p))],
        out_specs=pl.BlockSpec((1, nctx, LANES), lambda bi, p: (bi, 0, p)),
        compiler_params=_params("parallel", "parallel"),
        name="na_context",
    )(na, na, na)


def _mlap_kernel(mc_ref, cos_ref, sin_ref, gq_ref, gkv_ref, wq_ref, wqs_ref, wk_ref, wks_ref,
                 wv_ref, ones_ref, q_ref, k_ref, v_ref):
    mc = mc_ref[0]
    cq = mc[:, 0:MLA_Q_RANK].astype(F32)
    ckv = mc[:, MLA_Q_RANK:MLA_Q_RANK + MLA_KV_RANK].astype(F32)
    kr = mc[:, MLA_Q_RANK + MLA_KV_RANK:]
    nq = (cq * lax.rsqrt(jnp.mean(cq * cq, axis=-1, keepdims=True) + RMS_EPS) * gq_ref[...]).astype(BF16)
    nkv = (ckv * lax.rsqrt(jnp.mean(ckv * ckv, axis=-1, keepdims=True) + RMS_EPS) * gkv_ref[...]).astype(BF16)
    cos = cos_ref[...]
    sin = sin_ref[...]
    q1 = _dot(nq, wq_ref[...])
    q2 = _dot(nq, wqs_ref[...])
    k1 = _dot(nkv, wk_ref[0:MLA_KV_RANK, :]) + _dot(kr, wk_ref[MLA_KV_RANK:, :])
    k2 = _dot(kr, wks_ref[...])
    v = _dot(nkv, wv_ref[...]) + ones_ref[...]
    for h in range(MLA_HEADS):
        sl = slice(h * LANES, (h + 1) * LANES)
        q_ref[0, h] = (q1[:, sl] * cos + q2[:, sl] * sin).astype(BF16)
        k_ref[0, h] = (k1[:, sl] * cos + k2[:, sl] * sin).astype(BF16)
        v_ref[0, h] = v[:, sl].astype(BF16)


def _mlap_call(mc, cos_t, sin_t, gq, gkv, wq, wqs, wk, wks, wv, ones):
    b, ltot, _ = mc.shape
    tm = TOKEN_TILE
    const = lambda a: pl.BlockSpec(a.shape, lambda bi, i: (0,) * a.ndim)
    hm = jax.ShapeDtypeStruct((b, MLA_HEADS, ltot, LANES), BF16)
    hspec = pl.BlockSpec((1, MLA_HEADS, tm, LANES), lambda bi, i: (bi, 0, i, 0))
    return pl.pallas_call(
        _mlap_kernel,
        out_shape=(hm, hm, hm),
        grid=(b, ltot // tm),
        in_specs=[pl.BlockSpec((1, tm, mc.shape[2]), lambda bi, i: (bi, i, 0)),
                  pl.BlockSpec((tm, LANES), lambda bi, i: (i, 0)),
                  pl.BlockSpec((tm, LANES), lambda bi, i: (i, 0)),
                  const(gq), const(gkv), const(wq), const(wqs), const(wk), const(wks), const(wv), const(ones)],
        out_specs=(hspec, hspec, hspec),
        compiler_params=_params("parallel", "parallel"),
        name="mla_proj",
    )(mc, cos_t, sin_t, gq, gkv, wq, wqs, wk, wks, wv, ones)


def _fa_kernel(q_ref, k_ref, v_ref, o_ref, *, tq, tk, nk):
    lane = lax.broadcasted_iota(jnp.int32, (tq, LANES), 1)
    lo = lane < HALF
    outs = []
    for hh in range(2):
        q = q_ref[0, hh]

        def body(c, carry, hh=hh, q=q):
            m, acc = carry
            off = pl.multiple_of(c * tk, tk)
            kc = k_ref[0, hh, pl.ds(off, tk), :]
            vc = v_ref[0, hh, pl.ds(off, tk), :]
            s = _dot_nt(q, kc)
            mn = jnp.maximum(m, jnp.max(s, axis=-1, keepdims=True))
            p = jnp.exp2(s - mn).astype(BF16)
            acc = acc * jnp.exp2(m - mn) + _dot(p, vc)
            return mn, acc

        m0 = jnp.full((tq, 1), NEG_INF, F32)
        acc0 = jnp.zeros((tq, LANES), F32)
        _, acc = lax.fori_loop(0, nk, body, (m0, acc0))
        den = acc[:, HALF:HALF + 1] if hh == 0 else acc[:, 0:1]
        outs.append(acc / den)
    o_ref[0] = jnp.where(lo, outs[0], outs[1]).astype(BF16)


def _fa_call(q, k, v, nq_rows, q_row0, lk, k_row0, tq, tk):
    b, h, _, _ = q.shape
    npair = h // 2
    kern = functools.partial(_fa_kernel, tq=tq, tk=tk, nk=lk // tk)
    qb0 = q_row0 // tq
    kb0 = k_row0 // lk
    return pl.pallas_call(
        kern,
        out_shape=jax.ShapeDtypeStruct((b, nq_rows, h * HALF), BF16),
        grid=(b, npair, nq_rows // tq),
        in_specs=[pl.BlockSpec((1, 2, tq, LANES), lambda bi, p, i: (bi, p, qb0 + i, 0)),
                  pl.BlockSpec((1, 2, lk, LANES), lambda bi, p, i: (bi, p, kb0, 0)),
                  pl.BlockSpec((1, 2, lk, LANES), lambda bi, p, i: (bi, p, kb0, 0))],
        out_specs=pl.BlockSpec((1, tq, LANES), lambda bi, p, i: (bi, i, p)),
        compiler_params=_params("parallel", "parallel", "parallel"),
        name="mla_attn",
    )(q, k, v)


def _gelu_tanh(x):
    return 0.5 * x * (1.0 + jnp.tanh(math.sqrt(2.0 / math.pi) * (x + 0.044715 * (x * x * x))))


def _merge_kernel(y_ref, u_ref, z_ref, yn_ref, ym_ref, g_ref, x_ref, mod_ref, wglu_ref, bglu_ref,
                  d_ref, wa_ref, wb_ref, wc_ref, wo_ref, lng_ref, lnb_ref, o_ref, *, alpha, d_model, width):
    ya = _gelu_tanh(y_ref[0] + d_ref[...] * u_ref[0].astype(F32))
    ya = ya * _sigmoid(_dot(ya.astype(BF16), wglu_ref[...]) + bglu_ref[...])
    z = z_ref[0]
    g = g_ref[0]
    ba = _dot((ya * z[:, 0:width].astype(F32)).astype(BF16), wa_ref[...])
    bn = _dot(yn_ref[0] * z[:, width:2 * width], wb_ref[...])
    bm = _dot(ym_ref[0] * z[:, 2 * width:3 * width], wc_ref[...])
    m = (g[:, 0:d_model].astype(F32) * ba + g[:, d_model:2 * d_model].astype(F32) * bn
         + g[:, 2 * d_model:3 * d_model].astype(F32) * bm)
    out = _dot(m.astype(BF16), wo_ref[...])
    t = alpha * x_ref[0] + mod_ref[0, 0][2:3] * out
    mu = jnp.mean(t, axis=-1, keepdims=True)
    tc = t - mu
    var = jnp.mean(tc * tc, axis=-1, keepdims=True)
    o_ref[0] = tc * lax.rsqrt(var + LN_EPS) * lng_ref[...] + lnb_ref[...]


def _merge_call(y, u, z, yn, ym, g, xcat, modt, wglu, bglu, dskip, wa, wb, wc, wo, lng, lnb,
                n_lat_tiles, alpha):
    b, ltot, d = xcat.shape
    width = u.shape[2]
    tm = TOKEN_TILE
    tok = lambda a: pl.BlockSpec((1, tm, a.shape[2]), lambda bi, i: (bi, i, 0))
    const = lambda a: pl.BlockSpec(a.shape, lambda bi, i: (0,) * a.ndim)
    kern = functools.partial(_merge_kernel, alpha=alpha, d_model=d, width=width)
    return pl.pallas_call(
        kern,
        out_shape=jax.ShapeDtypeStruct((b, ltot, d), F32),
        grid=(b, ltot // tm),
        in_specs=[tok(y), tok(u), tok(z), tok(yn), tok(ym), tok(g), tok(xcat),
                  pl.BlockSpec((1, 1, 3, d), lambda bi, i: (bi, i // n_lat_tiles, 0, 0)),
                  const(wglu), const(bglu), const(dskip), const(wa), const(wb), const(wc), const(wo),
                  const(lng), const(lnb)],
        out_specs=pl.BlockSpec((1, tm, d), lambda bi, i: (bi, i, 0)),
        compiler_params=_params("parallel", "parallel"),
        name="merge_deepnorm",
    )(y, u, z, yn, ym, g, xcat, modt, wglu, bglu, dskip, wa, wb, wc, wo, lng, lnb)


def _prep_w_in(w_in):
    sizes = (512, 512, 512, 512, 512, 512, MLA_Q_RANK, MLA_KV_RANK, MLA_ROPE, 512, 1024, 1024, 1024)
    offs = np.cumsum((0,) + sizes)
    ua, za, qn, kn, vn, zn, cq, ckv, kr, zm, ga, gn, gm = [w_in[:, offs[i]:offs[i + 1]] for i in range(13)]
    pad = jnp.zeros((w_in.shape[0], LANES - MLA_ROPE), w_in.dtype)
    qn = qn * (NA_HEAD_DIM ** -0.5 * LOG2E)
    return jnp.concatenate([ua, za, zn, zm, qn, kn, vn, cq, ckv, kr, pad, ga, gn, gm], axis=1).astype(BF16)


_ROPE_SWAP = np.concatenate([np.arange(8, 16), np.arange(0, 8), np.arange(24, 32), np.arange(16, 24)])


def _prep_mla(w_uq, w_ukv):
    h = MLA_HEADS
    scale = MLA_QK ** -0.5 * LOG2E
    wq = w_uq.reshape(MLA_Q_RANK, h, MLA_QK) * scale
    zq = jnp.zeros((MLA_Q_RANK, h, LANES - MLA_QK), F32)
    nope, pe = wq[:, :, :MLA_NOPE], wq[:, :, MLA_NOPE:]
    wq_main = jnp.concatenate([nope, pe, zq], axis=2).reshape(MLA_Q_RANK, h * LANES)
    wq_swap = jnp.concatenate([jnp.zeros_like(nope), pe[:, :, _ROPE_SWAP], zq], axis=2).reshape(MLA_Q_RANK, h * LANES)

    wkv = w_ukv.reshape(MLA_KV_RANK, h, MLA_NOPE + MLA_V)
    knope, wv = wkv[:, :, :MLA_NOPE], wkv[:, :, MLA_NOPE:]
    zk = jnp.zeros((MLA_KV_RANK, h, LANES - MLA_NOPE), F32)
    wk_top = jnp.concatenate([knope, zk], axis=2).reshape(MLA_KV_RANK, h * LANES)
    eye = np.zeros((LANES, LANES), np.float32)
    eye[np.arange(MLA_ROPE), MLA_NOPE + np.arange(MLA_ROPE)] = 1.0
    eye_sw = np.zeros((LANES, LANES), np.float32)
    eye_sw[_ROPE_SWAP, MLA_NOPE + np.arange(MLA_ROPE)] = 1.0
    wk_bot = jnp.asarray(np.tile(eye, (1, h)))
    wk_swap = jnp.asarray(np.tile(eye_sw, (1, h)))
    wk_main = jnp.concatenate([wk_top, wk_bot], axis=0)

    zv = jnp.zeros_like(wv)
    even = jnp.concatenate([wv, zv], axis=2)
    odd = jnp.concatenate([zv, wv], axis=2)
    is_even = (np.arange(h) % 2 == 0)[None, :, None]
    wv_full = jnp.where(is_even, even, odd).reshape(MLA_KV_RANK, h * LANES)
    ones = np.zeros((1, h, LANES), np.float32)
    ones[0, 0::2, MLA_V] = 1.0
    ones[0, 1::2, 0] = 1.0
    return (wq_main.astype(BF16), wq_swap.astype(BF16), wk_main.astype(BF16), wk_swap.astype(BF16),
            wv_full.astype(BF16), jnp.asarray(ones.reshape(1, h * LANES)))


def _rope_tables(nlat, nctx):
    t = jnp.arange(nlat, dtype=jnp.int32)
    row = (t // GRID_W).astype(F32)
    col = (t % GRID_W).astype(F32)
    half = MLA_ROPE // 2
    inv = 1.0 / (ROPE_THETA ** (jnp.arange(0, half, 2, dtype=F32) / half))
    ar = row[:, None] * inv[None, :]
    ac = col[:, None] * inv[None, :]
    one = jnp.ones((nlat, MLA_NOPE), F32)
    tail = jnp.ones((nlat, LANES - MLA_QK), F32)
    cos_t = jnp.concatenate([one, jnp.cos(ar), jnp.cos(ar), jnp.cos(ac), jnp.cos(ac), tail], axis=1)
    sin_t = jnp.concatenate([0 * one, -jnp.sin(ar), jnp.sin(ar), -jnp.sin(ac), jnp.sin(ac), 0 * tail], axis=1)
    cos_t = jnp.concatenate([cos_t, jnp.ones((nctx, LANES), F32)], axis=0)
    sin_t = jnp.concatenate([sin_t, jnp.zeros((nctx, LANES), F32)], axis=0)
    return cos_t, sin_t


def _na_bias_tables(rpb):
    h = rpb.shape[0]
    wr, wc = NA_WIN_ROWS, NA_WIN_COLS
    col = np.arange(GRID_W)
    col_start = np.clip(col - wc // 2, 0, GRID_W - wc)
    in_win = (col[None, :] >= col_start[:, None]) & (col[None, :] < col_start[:, None] + wc)
    idx_c = np.clip(col[None, :] - col[:, None], -(wc - 1), wc - 1) + wc - 1
    idx_r = np.arange(wr)[None, :] - np.arange(wr)[:, None] + wr - 1
    b = rpb[:, idx_r[:, :, None, None], idx_c[None, None, :, :]]
    b = jnp.where(in_win[None, None, None], b * LOG2E, NEG_INF)
    return jnp.transpose(b, (0, 1, 3, 2, 4)).reshape(h, wr, GRID_W, wr * GRID_W).astype(F32)


def _s5_mats(lam_re, lam_im, log_dt, b_re, b_im, c_re, c_im):
    hp = lax.Precision.HIGHEST
    t = SSM_CHUNK
    g, p = lam_re.shape[1], lam_re.shape[2]
    gi = b_re.shape[-1]
    dt = jnp.exp(log_dt.astype(F32))[..., None]
    lr, li = lam_re.astype(F32), lam_im.astype(F32)
    mag = jnp.exp(lr * dt)
    ab_re, ab_im = mag * jnp.cos(li * dt), mag * jnp.sin(li * dt)
    nr = ab_re - 1.0
    den = lr * lr + li * li
    fr = (nr * lr + ab_im * li) / den
    fi = (ab_im * lr - nr * li) / den
    bb_re = fr[..., None] * b_re - fi[..., None] * b_im
    bb_im = fr[..., None] * b_im + fi[..., None] * b_re
    j = jnp.arange(t + 1, dtype=F32)[None, None, :, None]
    pmag = jnp.exp(lr[:, :, None, :] * dt[:, :, None, :] * j)
    ang = li[:, :, None, :] * dt[:, :, None, :] * j
    ap_re, ap_im = pmag * jnp.cos(ang), pmag * jnp.sin(ang)

    ca_re = c_re[:, :, None] * ap_re[:, :, :, None, :] - c_im[:, :, None] * ap_im[:, :, :, None, :]
    ca_im = c_re[:, :, None] * ap_im[:, :, :, None, :] + c_im[:, :, None] * ap_re[:, :, :, None, :]
    kmat = (jnp.einsum('dgjip,dgpk->dgjik', ca_re, bb_re, precision=hp)
            - jnp.einsum('dgjip,dgpk->dgjik', ca_im, bb_im, precision=hp))
    tt = np.arange(t)
    lag_f = tt[None, :] - tt[:, None]
    kf = jnp.where((lag_f >= 0)[None, :, :, None, None], kmat[0][:, np.clip(lag_f, 0, t)], 0.0)
    kb = jnp.where((lag_f <= 0)[None, :, :, None, None], kmat[1][:, np.clip(-lag_f, 0, t)], 0.0)
    toep = jnp.transpose(kf + kb, (0, 1, 4, 2, 3)).reshape(g, t * gi, t * gi)

    def end_map(d, powers):
        are, aim = ap_re[d][:, powers], ap_im[d][:, powers]
        bre, bim = jnp.transpose(bb_re[d], (0, 2, 1)), jnp.transpose(bb_im[d], (0, 2, 1))
        mre = are[:, :, None, :] * bre[:, None] - aim[:, :, None, :] * bim[:, None]
        mim = are[:, :, None, :] * bim[:, None] + aim[:, :, None, :] * bre[:, None]
        return mre.reshape(g, t * gi, p), mim.reshape(g, t * gi, p)

    ef_re, ef_im = end_map(0, t - 1 - tt)
    eb_re, eb_im = end_map(1, tt)

    def out_map(d, powers):
        cre = jnp.transpose(ca_re[d][:, powers], (0, 3, 1, 2)).reshape(g, p, t * gi)
        cim = jnp.transpose(ca_im[d][:, powers], (0, 3, 1, 2)).reshape(g, p, t * gi)
        return cre, -cim

    of_re, of_im = out_map(0, tt + 1)
    ob_re, ob_im = out_map(1, t - tt)

    npair = g // 2
    z256 = jnp.zeros((npair, t * gi, t * gi), F32)
    z64 = jnp.zeros((npair, t * gi, p), F32)

    def pair_cols(m):
        m0, m1 = m[0::2], m[1::2]
        return jnp.concatenate([jnp.concatenate([m0, z64], axis=2), jnp.concatenate([z64, m1], axis=2)], axis=1)

    t0, t1 = toep[0::2], toep[1::2]
    ypart = jnp.concatenate([jnp.concatenate([t0, z256], axis=2), jnp.concatenate([z256, t1], axis=2)], axis=1)
    w_pairs = jnp.concatenate([ypart, pair_cols(ef_re), pair_cols(ef_im), pair_cols(eb_re), pair_cols(eb_im)],
                              axis=2)

    zo = jnp.zeros((npair, p, t * gi), F32)

    def pair_rows(m):
        m0, m1 = m[0::2], m[1::2]
        return jnp.concatenate([jnp.concatenate([m0, zo], axis=2), jnp.concatenate([zo, m1], axis=2)], axis=1)

    mo_pairs = jnp.concatenate([pair_rows(of_re), pair_rows(of_im), pair_rows(ob_re), pair_rows(ob_im)], axis=1)

    def pair_vec(v):
        return jnp.concatenate([v[0::2], v[1::2]], axis=1)

    a_pairs = jnp.stack([pair_vec(ap_re[0][:, t]), pair_vec(ap_im[0][:, t]),
                         pair_vec(ap_re[1][:, t]), pair_vec(ap_im[1][:, t])], axis=1)
    return w_pairs.astype(BF16), mo_pairs.astype(BF16), a_pairs


def kernel(x, c, ctx, c_ctx, w_mod, b_mod, w_in, ssm_lam_re, ssm_lam_im, ssm_log_dt, ssm_b_re, ssm_b_im,
           ssm_c_re, ssm_c_im, ssm_d, ssm_w_glu, ssm_b_glu, na_rpb, mla_q_norm, mla_w_uq, mla_kv_norm,
           mla_w_ukv, w_branch_a, w_branch_b, w_branch_c, w_out, ln_g, ln_b):
    b, nlat, d = x.shape
    nctx = ctx.shape[1]
    ltot = nlat + nctx
    depth = w_mod.shape[0]
    assert nlat % TOKEN_TILE == 0 and nctx % TOKEN_TILE == 0 and nlat % nctx == 0
    assert nlat % (NA_ROW_BLOCK * GRID_W) == 0 and b + 1 <= 8
    assert nlat % (8 * SSM_CHUNK) == 0 and nctx % (8 * SSM_CHUNK) == 0
    n_lat_tiles = nlat // TOKEN_TILE
    alpha = (2 * depth) ** 0.25
    ngroups = ssm_lam_re.shape[2]
    npair = ngroups // 2
    nch_lat, nch_ctx = nlat // SSM_CHUNK, nctx // SSM_CHUNK
    nch = nch_lat + nch_ctx
    fa_tk = FA_TK if ltot % FA_TK == 0 else nctx

    cc = jnp.concatenate([c, c_ctx[None], jnp.zeros((8 - b - 1, d), F32)], axis=0)
    mod_all = _mod_call(cc, w_mod, b_mod)
    xcat = jnp.concatenate([x, ctx], axis=1)
    cos_t, sin_t = _rope_tables(nlat, nctx)

    for l in range(depth):
        mod_lat = mod_all[l, :b].reshape(b, 1, 3, d)
        mod_ctx = jnp.broadcast_to(mod_all[l, b].reshape(1, 1, 3, d), (b, 1, 3, d))
        modt = jnp.concatenate([mod_lat, mod_ctx], axis=1)

        u, z, na, mc, g = _in_call(xcat, modt, _prep_w_in(w_in[l]), n_lat_tiles)

        w_pairs, mo_pairs, a_pairs = _s5_mats(ssm_lam_re[l], ssm_lam_im[l], ssm_log_dt[l], ssm_b_re[l],
                                              ssm_b_im[l], ssm_c_re[l], ssm_c_im[l])
        u_pairs = jnp.transpose(u.reshape(b, nch, SSM_CHUNK, npair, 2, SSM_GROUP), (3, 0, 1, 4, 2, 5))
        u_pairs = u_pairs.reshape(npair, b * nch, 2 * SSM_CHUNK * SSM_GROUP)
        y_pairs = _s5_call(u_pairs, w_pairs, mo_pairs, a_pairs, b, nch_lat, nch_ctx)
        y = jnp.transpose(y_pairs.reshape(npair, b, nch, 2, SSM_CHUNK, SSM_GROUP), (1, 2, 4, 0, 3, 5))
        y = y.reshape(b, ltot, ngroups * SSM_GROUP)

        yn = jnp.concatenate([_na_call(na, _na_bias_tables(na_rpb[l]), nlat, nctx),
                              _na_ctx_call(na, nlat, nctx)], axis=1)

        wq, wqs, wk, wks, wv, ones = _prep_mla(mla_w_uq[l], mla_w_ukv[l])
        q, k, v = _mlap_call(mc, cos_t, sin_t, mla_q_norm[l][None], mla_kv_norm[l][None],
                             wq, wqs, wk, wks, wv, ones)
        ym = jnp.concatenate([_fa_call(q, k, v, nlat, 0, ltot, 0, FA_TQ, fa_tk),
                              _fa_call(q, k, v, nctx, nlat, nctx, nlat, nctx, nctx)], axis=1)

        xcat = _merge_call(y, u, z, yn, ym, g, xcat, modt,
                           ssm_w_glu[l].astype(BF16), ssm_b_glu[l][None], ssm_d[l][None],
                           w_branch_a[l].astype(BF16), w_branch_b[l].astype(BF16), w_branch_c[l].astype(BF16),
                           w_out[l].astype(BF16), ln_g[l][None], ln_b[l][None], n_lat_tiles, alpha)
    return xcat[:, :nlat]
```

```python
import functools
import math

import jax
import jax.numpy as jnp
import numpy as np
from jax import lax
from jax.experimental import pallas as pl
from jax.experimental.pallas import tpu as pltpu

F32 = jnp.float32
BF16 = jnp.bfloat16

GRID_W = 64
SSM_GROUP = 16
SSM_STATE = 64
SSM_CHUNK = 16
NA_HEADS = 8
NA_HEAD_DIM = 64
NA_WIN_ROWS = 8
NA_WIN_COLS = 16
MLA_HEADS = 8
MLA_Q_RANK = 256
MLA_KV_RANK = 128
MLA_NOPE = 64
MLA_ROPE = 32
MLA_V = 64
MLA_QK = MLA_NOPE + MLA_ROPE
ROPE_THETA = 10000.0
LN_EPS = 1e-5
RMS_EPS = 1e-6
NEG_INF = -1e30
LOG2E = math.log2(math.e)
LANES = 128
HALF = 64

TOKEN_TILE = 256
NA_ROW_BLOCK = 8
FA_TQ = 1024
FA_TK = 2816
VMEM_LIMIT = 56 * 1024 * 1024


def _sigmoid(x):
    return 1.0 / (1.0 + jnp.exp(-x))


def _dot(a, b):
    return jnp.dot(a, b, preferred_element_type=F32)


def _dot_nt(a, b):
    return lax.dot_general(a, b, (((1,), (1,)), ((), ())), preferred_element_type=F32)


def _params(*sem):
    return pltpu.CompilerParams(dimension_semantics=sem, vmem_limit_bytes=VMEM_LIMIT)


def _layer_spec(arr, layer, grid_rank):
    zeros = (0,) * (arr.ndim - 1)
    if grid_rank == 2:
        imap = lambda a, b: (layer,) + zeros
    else:
        imap = lambda a, b, c: (layer,) + zeros
    return pl.BlockSpec((None,) + arr.shape[1:], imap)


def _mod_kernel(cc_ref, w_ref, b_ref, o_ref):
    cc = cc_ref[...]
    s = (cc * _sigmoid(cc)).astype(BF16)
    o_ref[0] = _dot(s, w_ref[0].astype(BF16)) + b_ref[0]


def _mod_call(cc, w_mod, b_mod):
    depth, d, n = w_mod.shape
    tn = 1024
    return pl.pallas_call(
        _mod_kernel,
        out_shape=jax.ShapeDtypeStruct((depth, 8, n), F32),
        grid=(depth, n // tn),
        in_specs=[pl.BlockSpec((8, d), lambda l, j: (0, 0)),
                  pl.BlockSpec((1, d, tn), lambda l, j: (l, 0, j)),
                  pl.BlockSpec((1, 1, tn), lambda l, j: (l, 0, j))],
        out_specs=pl.BlockSpec((1, 8, tn), lambda l, j: (l, 0, j)),
        compiler_params=_params("parallel", "parallel"),
        name="adaln_mod",
    )(cc, w_mod, b_mod.reshape(depth, 1, n))


IN_WIDTH_PADDED = 7168
OUT_COLS = (512, 1536, 1536, 512, 3072)


def _in_kernel(x_ref, mod_ref, w_ref, o_u, o_z, o_na, o_mc, o_g):
    x = x_ref[0]
    mu = jnp.mean(x, axis=-1, keepdims=True)
    xc = x - mu
    var = jnp.mean(xc * xc, axis=-1, keepdims=True)
    mod = mod_ref[0, 0]
    h = (xc * lax.rsqrt(var + LN_EPS) * (1.0 + mod[1:2]) + mod[0:1]).astype(BF16)

    def mm(lo, hi):
        return _dot(h, w_ref[:, lo:hi])

    def silu(lo, hi):
        z = mm(lo, hi)
        return (z * _sigmoid(z)).astype(BF16)

    o_u[0] = mm(0, 512)
    o_z[0, :, 0:512] = silu(512, 1024)
    o_na[0] = mm(1024, 2560).astype(BF16)
    o_z[0, :, 512:1024] = silu(2560, 3072)
    o_mc[0] = mm(3072, 3584).astype(BF16)
    o_z[0, :, 1024:1536] = silu(3584, 4096)
    o_g[0] = _sigmoid(mm(4096, 7168)).astype(BF16)


def _in_call(xcat, modt, w_pad, n_lat_tiles, layer):
    b, ltot, d = xcat.shape
    tm = TOKEN_TILE
    tok = lambda w: pl.BlockSpec((1, tm, w), lambda bi, i: (bi, i, 0))
    dts = (F32, BF16, BF16, BF16, BF16)
    return pl.pallas_call(
        _in_kernel,
        out_shape=tuple(jax.ShapeDtypeStruct((b, ltot, w), dt) for w, dt in zip(OUT_COLS, dts)),
        grid=(b, ltot // tm),
        in_specs=[tok(d),
                  pl.BlockSpec((None, 1, 1, 3, d), lambda bi, i: (layer, bi, i // n_lat_tiles, 0, 0)),
                  pl.BlockSpec((None,) + w_pad.shape[1:], lambda bi, i: (layer, 0, 0),
                               pipeline_mode=pl.Buffered(1))],
        out_specs=tuple(tok(w) for w in OUT_COLS),
        compiler_params=_params("parallel", "parallel"),
        name="ln_in_proj",
    )(xcat, modt, w_pad)


S5_SUB = 8
S5_STATE_COLS = (LANES // SSM_GROUP) * SSM_STATE


def _s5_gather_chunks(u_ref, nch):
    cols = [u_ref[0, pl.ds(t, nch, stride=SSM_CHUNK), :].astype(BF16) for t in range(SSM_CHUNK)]
    return jnp.concatenate(cols, axis=1)


def _s5_state_kernel(u_ref, we_ref, a_ref, s_ref, e_scr, s_scr, *, nlat, nctx):
    ntot = nlat + nctx
    w = S5_STATE_COLS
    e_scr[...] = _dot(_s5_gather_chunks(u_ref, ntot), we_ref[0])
    a = a_ref[0]
    arf, aif, arb, aib = a[0:1], a[1:2], a[2:3], a[3:4]
    ntile, nctx_t, nlat_t = ntot // S5_SUB, nctx // S5_SUB, nlat // S5_SUB

    def sweep(er, ei, sr, si, ar, ai, order):
        rows_r, rows_i = [None] * S5_SUB, [None] * S5_SUB
        for k in order:
            rows_r[k], rows_i[k] = sr, si
            sr, si = ar * sr - ai * si + er[k:k + 1], ar * si + ai * sr + ei[k:k + 1]
        return jnp.concatenate(rows_r, axis=0), jnp.concatenate(rows_i, axis=0), sr, si

    def body(i, carry):
        srf, sif, srb, sib = carry
        rf = pl.multiple_of(jnp.where(i < nctx_t, nlat_t + i, i - nctx_t) * S5_SUB, S5_SUB)
        rb = pl.multiple_of((ntile - 1 - i) * S5_SUB, S5_SUB)
        pfr, pfi, srf, sif = sweep(e_scr[pl.ds(rf, S5_SUB), 0:w], e_scr[pl.ds(rf, S5_SUB), w:2 * w],
                                   srf, sif, arf, aif, range(S5_SUB))
        pbr, pbi, srb, sib = sweep(e_scr[pl.ds(rb, S5_SUB), 2 * w:3 * w], e_scr[pl.ds(rb, S5_SUB), 3 * w:4 * w],
                                   srb, sib, arb, aib, range(S5_SUB - 1, -1, -1))
        s_scr[pl.ds(rf, S5_SUB), 0:w] = pfr
        s_scr[pl.ds(rf, S5_SUB), w:2 * w] = pfi
        s_scr[pl.ds(rb, S5_SUB), 2 * w:3 * w] = pbr
        s_scr[pl.ds(rb, S5_SUB), 3 * w:4 * w] = pbi
        return srf, sif, srb, sib

    zero = jnp.zeros((1, w), F32)
    lax.fori_loop(0, ntile, body, (zero, zero, zero, zero))
    s_ref[0, 0] = s_scr[...].astype(BF16)


def _s5_out_kernel(u_ref, s_ref, wy_ref, mo_ref, y_ref, *, nch):
    lhs = _s5_gather_chunks(u_ref, nch)
    s = s_ref[0, 0]
    half = SSM_CHUNK // 2
    for hf in range(2):
        cols = slice(hf * half * LANES, (hf + 1) * half * LANES)
        y = _dot(lhs, wy_ref[0, :, cols]) + _dot(s, mo_ref[0, :, cols])
        for t in range(half):
            y_ref[0, pl.ds(hf * half + t, nch, stride=SSM_CHUNK), :] = y[:, t * LANES:(t + 1) * LANES]


def _s5_call(u, we, wy, mo, a, nlat, nctx, layer):
    b, ltot, width = u.shape
    nblk = width // LANES
    nch = ltot // SSM_CHUNK
    kdim = SSM_CHUNK * LANES
    scols = 4 * S5_STATE_COLS
    ublk = pl.BlockSpec((1, ltot, LANES), lambda j, bi: (bi, 0, j))
    sblk = pl.BlockSpec((1, 1, nch, scols), lambda j, bi: (bi, j, 0, 0))
    wblk = lambda k, n: pl.BlockSpec((1, k, n), lambda j, bi: (layer * nblk + j, 0, 0),
                                     pipeline_mode=pl.Buffered(1))
    states = pl.pallas_call(
        functools.partial(_s5_state_kernel, nlat=nlat // SSM_CHUNK, nctx=nctx // SSM_CHUNK),
        out_shape=jax.ShapeDtypeStruct((b, nblk, nch, scols), BF16),
        grid=(nblk, b),
        in_specs=[ublk, wblk(kdim, scols),
                  pl.BlockSpec((1, 4, S5_STATE_COLS), lambda j, bi: (layer * nblk + j, 0, 0))],
        out_specs=sblk,
        scratch_shapes=[pltpu.VMEM((nch, scols), F32), pltpu.VMEM((nch, scols), F32)],
        compiler_params=_params("parallel", "parallel"),
        name="s5_states",
    )(u, we, a)
    return pl.pallas_call(
        functools.partial(_s5_out_kernel, nch=nch),
        out_shape=jax.ShapeDtypeStruct((b, ltot, width), F32),
        grid=(nblk, b),
        in_specs=[ublk, sblk, wblk(kdim, kdim), wblk(scols, kdim)],
        out_specs=ublk,
        compiler_params=_params("parallel", "parallel"),
        name="s5_outputs",
    )(u, states, wy, mo)


def _na_kernel(q_ref, k_ref, v_ref, bias_ref, o_ref, *, rb, nrows, nlat, nctx):
    i = pl.program_id(2)
    lane = lax.broadcasted_iota(jnp.int32, (GRID_W, LANES), 1)
    lo = lane < HALF
    hi = lane >= HALF
    kc = k_ref[0, nlat:nlat + nctx, :]
    vc = v_ref[0, nlat:nlat + nctx, :]
    win = NA_WIN_ROWS * GRID_W
    for j in range(rb):
        r = i * rb + j
        start = jnp.clip(r - NA_WIN_ROWS // 2, 0, nrows - NA_WIN_ROWS)
        delta = r - start
        off = pl.multiple_of(start * GRID_W, GRID_W)
        kw = k_ref[0, pl.ds(off, win), :]
        vw = v_ref[0, pl.ds(off, win), :]
        q = q_ref[0, j * GRID_W:(j + 1) * GRID_W, :]
        outs = []
        for hh in range(2):
            qm = jnp.where(lo if hh == 0 else hi, q, jnp.zeros_like(q))
            sl = _dot_nt(qm, kw) + bias_ref[hh, delta]
            sc = _dot_nt(qm, kc)
            m = jnp.maximum(jnp.max(sl, axis=-1, keepdims=True), jnp.max(sc, axis=-1, keepdims=True))
            pw = jnp.exp2(sl - m)
            pc = jnp.exp2(sc - m)
            den = jnp.sum(pw, axis=-1, keepdims=True) + jnp.sum(pc, axis=-1, keepdims=True)
            o = _dot(pw.astype(BF16), vw) + _dot(pc.astype(BF16), vc)
            outs.append(o / den)
        o_ref[0, j * GRID_W:(j + 1) * GRID_W, :] = jnp.where(lo, outs[0], outs[1]).astype(BF16)


def _na_call(na, bias_all, nlat, nctx, layer):
    b, ltot, _ = na.shape
    nrows = nlat // GRID_W
    rb = NA_ROW_BLOCK
    npair = NA_HEADS // 2
    kern = functools.partial(_na_kernel, rb=rb, nrows=nrows, nlat=nlat, nctx=nctx)
    return pl.pallas_call(
        kern,
        out_shape=jax.ShapeDtypeStruct((b, nlat, NA_HEADS * NA_HEAD_DIM), BF16),
        grid=(b, npair, nrows // rb),
        in_specs=[pl.BlockSpec((1, rb * GRID_W, LANES), lambda bi, p, i: (bi, i, p)),
                  pl.BlockSpec((1, ltot, LANES), lambda bi, p, i: (bi, 0, npair + p)),
                  pl.BlockSpec((1, ltot, LANES), lambda bi, p, i: (bi, 0, 2 * npair + p)),
                  pl.BlockSpec((2, NA_WIN_ROWS, GRID_W, NA_WIN_ROWS * GRID_W),
                               lambda bi, p, i: (layer * npair + p, 0, 0, 0))],
        out_specs=pl.BlockSpec((1, rb * GRID_W, LANES), lambda bi, p, i: (bi, i, p)),
        compiler_params=_params("parallel", "parallel", "parallel"),
        name="na_latent",
    )(na, na, na, bias_all)


def _na_ctx_kernel(q_ref, k_ref, v_ref, o_ref):
    n = q_ref.shape[1]
    lane = lax.broadcasted_iota(jnp.int32, (n, LANES), 1)
    lo = lane < HALF
    hi = lane >= HALF
    q = q_ref[0]
    k = k_ref[0]
    v = v_ref[0]
    outs = []
    for hh in range(2):
        qm = jnp.where(lo if hh == 0 else hi, q, jnp.zeros_like(q))
        s = _dot_nt(qm, k)
        p = jnp.exp2(s - jnp.max(s, axis=-1, keepdims=True))
        outs.append(_dot(p.astype(BF16), v) / jnp.sum(p, axis=-1, keepdims=True))
    o_ref[0] = jnp.where(lo, outs[0], outs[1]).astype(BF16)


def _na_ctx_call(na, nlat, nctx):
    b = na.shape[0]
    npair = NA_HEADS // 2
    rblk = nlat // nctx
    return pl.pallas_call(
        _na_ctx_kernel,
        out_shape=jax.ShapeDtypeStruct((b, nctx, NA_HEADS * NA_HEAD_DIM), BF16),
        grid=(b, npair),
        in_specs=[pl.BlockSpec((1, nctx, LANES), lambda bi, p: (bi, rblk, p)),
                  pl.BlockSpec((1, nctx, LANES), lambda bi, p: (bi, rblk, npair + p)),
                  pl.BlockSpec((1, nctx, LANES), lambda bi, p: (bi, rblk, 2 * npair + p))],
        out_specs=pl.BlockSpec((1, nctx, LANES), lambda bi, p: (bi, 0, p)),
        compiler_params=_params("parallel", "parallel"),
        name="na_context",
    )(na, na, na)


def _mlap_kernel(mc_ref, cos_ref, sin_ref, gq_ref, gkv_ref, wq_ref, wqs_ref, wk_ref, wks_ref,
                 wv_ref, ones_ref, q_ref, k_ref, vt_ref):
    mc = mc_ref[0]
    cq = mc[:, 0:MLA_Q_RANK].astype(F32)
    ckv = mc[:, MLA_Q_RANK:MLA_Q_RANK + MLA_KV_RANK].astype(F32)
    kr = mc[:, MLA_Q_RANK + MLA_KV_RANK:]
    nq = (cq * lax.rsqrt(jnp.mean(cq * cq, axis=-1, keepdims=True) + RMS_EPS) * gq_ref[...]).astype(BF16)
    nkv = (ckv * lax.rsqrt(jnp.mean(ckv * ckv, axis=-1, keepdims=True) + RMS_EPS) * gkv_ref[...]).astype(BF16)
    cos = cos_ref[...]
    sin = sin_ref[...]
    q1 = _dot(nq, wq_ref[...])
    q2 = _dot(nq, wqs_ref[...])
    k1 = _dot(nkv, wk_ref[0:MLA_KV_RANK, :]) + _dot(kr, wk_ref[MLA_KV_RANK:, :])
    k2 = _dot(kr, wks_ref[...])
    v = _dot(nkv, wv_ref[...]) + ones_ref[...]
    for h in range(MLA_HEADS):
        sl = slice(h * LANES, (h + 1) * LANES)
        q_ref[0, h] = (q1[:, sl] * cos + q2[:, sl] * sin).astype(BF16)
        k_ref[0, h] = (k1[:, sl] * cos + k2[:, sl] * sin).astype(BF16)
        vt_ref[0, h] = v[:, sl].T.astype(BF16)


def _mlap_call(mc, cos_t, sin_t, gq, gkv, wq, wqs, wk, wks, wv, ones, layer):
    b, ltot, _ = mc.shape
    tm = TOKEN_TILE
    lay = lambda a: _layer_spec(a, layer, 2)
    hm = jax.ShapeDtypeStruct((b, MLA_HEADS, ltot, LANES), BF16)
    hspec = pl.BlockSpec((1, MLA_HEADS, tm, LANES), lambda bi, i: (bi, 0, i, 0))
    return pl.pallas_call(
        _mlap_kernel,
        out_shape=(hm, hm, jax.ShapeDtypeStruct((b, MLA_HEADS, LANES, ltot), BF16)),
        grid=(b, ltot // tm),
        in_specs=[pl.BlockSpec((1, tm, mc.shape[2]), lambda bi, i: (bi, i, 0)),
                  pl.BlockSpec((tm, LANES), lambda bi, i: (i, 0)),
                  pl.BlockSpec((tm, LANES), lambda bi, i: (i, 0)),
                  lay(gq), lay(gkv), lay(wq), lay(wqs), lay(wk), lay(wks), lay(wv),
                  pl.BlockSpec(ones.shape, lambda bi, i: (0, 0))],
        out_specs=(hspec, hspec, pl.BlockSpec((1, MLA_HEADS, LANES, tm), lambda bi, i: (bi, 0, 0, i))),
        compiler_params=_params("parallel", "parallel"),
        name="mla_proj",
    )(mc, cos_t, sin_t, gq, gkv, wq, wqs, wk, wks, wv, ones)


def _fa_kernel(q_ref, k_ref, vt_ref, o_ref, *, tq, tk, nk):
    qs = [q_ref[0, hh] for hh in range(2)]

    def body(c, carry):
        off = pl.multiple_of(c * tk, tk)
        new = []
        for hh in range(2):
            m, acc = carry[2 * hh], carry[2 * hh + 1]
            kc = k_ref[0, hh, pl.ds(off, tk), :]
            vc = vt_ref[0, hh, :, pl.ds(off, tk)]
            s = _dot_nt(kc, qs[hh])
            mn = jnp.maximum(m, jnp.max(s, axis=0, keepdims=True))
            p = jnp.exp2(s - mn).astype(BF16)
            new += [mn, acc * jnp.exp2(m - mn) + _dot(vc, p)]
        return tuple(new)

    m0 = jnp.full((1, tq), NEG_INF, F32)
    acc0 = jnp.zeros((LANES, tq), F32)
    res = lax.fori_loop(0, nk, body, (m0, acc0, m0, acc0))
    o0 = res[1][0:HALF] / res[1][HALF:HALF + 1]
    o1 = res[3][HALF:LANES] / res[3][0:1]
    o_ref[0] = jnp.concatenate([o0, o1], axis=0).T.astype(BF16)


def _fa_call(q, k, vt, nq_rows, q_row0, lk, k_row0, tq, tk):
    b, h, _, _ = q.shape
    npair = h // 2
    kern = functools.partial(_fa_kernel, tq=tq, tk=tk, nk=lk // tk)
    qb0 = q_row0 // tq
    kb0 = k_row0 // lk
    return pl.pallas_call(
        kern,
        out_shape=jax.ShapeDtypeStruct((b, nq_rows, h * HALF), BF16),
        grid=(b, npair, nq_rows // tq),
        in_specs=[pl.BlockSpec((1, 2, tq, LANES), lambda bi, p, i: (bi, p, qb0 + i, 0)),
                  pl.BlockSpec((1, 2, lk, LANES), lambda bi, p, i: (bi, p, kb0, 0)),
                  pl.BlockSpec((1, 2, LANES, lk), lambda bi, p, i: (bi, p, 0, kb0))],
        out_specs=pl.BlockSpec((1, tq, LANES), lambda bi, p, i: (bi, i, p)),
        compiler_params=_params("parallel", "parallel", "parallel"),
        name="mla_attn",
    )(q, k, vt)


def _gelu_tanh(x):
    return 0.5 * x * (1.0 + jnp.tanh(math.sqrt(2.0 / math.pi) * (x + 0.044715 * (x * x * x))))


def _merge_kernel(y_ref, u_ref, z_ref, yn_ref, ym_ref, g_ref, x_ref, mod_ref, wglu_ref, bglu_ref,
                  d_ref, wa_ref, wb_ref, wc_ref, wo_ref, lng_ref, lnb_ref, o_ref, *, alpha, d_model, width):
    ya = _gelu_tanh(y_ref[0] + d_ref[...] * u_ref[0])
    ya = ya * _sigmoid(_dot(ya.astype(BF16), wglu_ref[...]) + bglu_ref[...])
    z = z_ref[0]
    g = g_ref[0]
    ba = _dot((ya * z[:, 0:width].astype(F32)).astype(BF16), wa_ref[...])
    bn = _dot(yn_ref[0] * z[:, width:2 * width], wb_ref[...])
    bm = _dot(ym_ref[0] * z[:, 2 * width:3 * width], wc_ref[...])
    m = (g[:, 0:d_model].astype(F32) * ba + g[:, d_model:2 * d_model].astype(F32) * bn
         + g[:, 2 * d_model:3 * d_model].astype(F32) * bm)
    out = _dot(m.astype(BF16), wo_ref[...])
    t = alpha * x_ref[0] + mod_ref[0, 0][2:3] * out
    mu = jnp.mean(t, axis=-1, keepdims=True)
    tc = t - mu
    var = jnp.mean(tc * tc, axis=-1, keepdims=True)
    o_ref[0] = tc * lax.rsqrt(var + LN_EPS) * lng_ref[...] + lnb_ref[...]


def _merge_call(y, u, z, yn, ym, g, xcat, modt, wglu, bglu, dskip, wa, wb, wc, wo, lng, lnb,
                n_lat_tiles, alpha, layer, n_out_tiles):
    b, _, d = xcat.shape
    width = u.shape[2]
    tm = TOKEN_TILE
    tok = lambda a: pl.BlockSpec((1, tm, a.shape[2]), lambda bi, i: (bi, i, 0))
    lay = lambda a: _layer_spec(a, layer, 2)
    kern = functools.partial(_merge_kernel, alpha=alpha, d_model=d, width=width)
    return pl.pallas_call(
        kern,
        out_shape=jax.ShapeDtypeStruct((b, n_out_tiles * tm, d), F32),
        grid=(b, n_out_tiles),
        in_specs=[tok(y), tok(u), tok(z), tok(yn), tok(ym), tok(g), tok(xcat),
                  pl.BlockSpec((None, 1, 1, 3, d), lambda bi, i: (layer, bi, i // n_lat_tiles, 0, 0)),
                  lay(wglu), lay(bglu), lay(dskip), lay(wa), lay(wb), lay(wc), lay(wo), lay(lng), lay(lnb)],
        out_specs=pl.BlockSpec((1, tm, d), lambda bi, i: (bi, i, 0)),
        compiler_params=_params("parallel", "parallel"),
        name="merge_deepnorm",
    )(y, u, z, yn, ym, g, xcat, modt, wglu, bglu, dskip, wa, wb, wc, wo, lng, lnb)


IN_PAD_AT = 6 * 512 + MLA_Q_RANK + MLA_KV_RANK + MLA_ROPE


def _prep_w_in(w_in):
    pad = jnp.zeros(w_in.shape[:-1] + (LANES - MLA_ROPE,), w_in.dtype)
    scale = np.ones((IN_WIDTH_PADDED,), np.float32)
    scale[1024:1536] = NA_HEAD_DIM ** -0.5 * LOG2E
    w = jnp.concatenate([w_in[..., :IN_PAD_AT], pad, w_in[..., IN_PAD_AT:]], axis=-1)
    return (w * scale).astype(BF16)


_ROPE_SWAP = np.concatenate([np.arange(8, 16), np.arange(0, 8), np.arange(24, 32), np.arange(16, 24)])


def _prep_mla(w_uq, w_ukv):
    h = MLA_HEADS
    scale = MLA_QK ** -0.5 * LOG2E
    wq = w_uq.reshape(MLA_Q_RANK, h, MLA_QK) * scale
    zq = jnp.zeros((MLA_Q_RANK, h, LANES - MLA_QK), F32)
    nope, pe = wq[:, :, :MLA_NOPE], wq[:, :, MLA_NOPE:]
    wq_main = jnp.concatenate([nope, pe, zq], axis=2).reshape(MLA_Q_RANK, h * LANES)
    wq_swap = jnp.concatenate([jnp.zeros_like(nope), pe[:, :, _ROPE_SWAP], zq], axis=2).reshape(MLA_Q_RANK, h * LANES)

    wkv = w_ukv.reshape(MLA_KV_RANK, h, MLA_NOPE + MLA_V)
    knope, wv = wkv[:, :, :MLA_NOPE], wkv[:, :, MLA_NOPE:]
    zk = jnp.zeros((MLA_KV_RANK, h, LANES - MLA_NOPE), F32)
    wk_top = jnp.concatenate([knope, zk], axis=2).reshape(MLA_KV_RANK, h * LANES)
    eye = np.zeros((LANES, LANES), np.float32)
    eye[np.arange(MLA_ROPE), MLA_NOPE + np.arange(MLA_ROPE)] = 1.0
    eye_sw = np.zeros((LANES, LANES), np.float32)
    eye_sw[_ROPE_SWAP, MLA_NOPE + np.arange(MLA_ROPE)] = 1.0
    wk_bot = jnp.asarray(np.tile(eye, (1, h)))
    wk_swap = jnp.asarray(np.tile(eye_sw, (1, h)))
    wk_main = jnp.concatenate([wk_top, wk_bot], axis=0)

    zv = jnp.zeros_like(wv)
    even = jnp.concatenate([wv, zv], axis=2)
    odd = jnp.concatenate([zv, wv], axis=2)
    is_even = (np.arange(h) % 2 == 0)[None, :, None]
    wv_full = jnp.where(is_even, even, odd).reshape(MLA_KV_RANK, h * LANES)
    return (wq_main.astype(BF16), wq_swap.astype(BF16), wk_main.astype(BF16), wk_swap.astype(BF16),
            wv_full.astype(BF16))


def _mla_ones_row():
    ones = np.zeros((1, MLA_HEADS, LANES), np.float32)
    ones[0, 0::2, MLA_V] = 1.0
    ones[0, 1::2, 0] = 1.0
    return jnp.asarray(ones.reshape(1, MLA_HEADS * LANES))


def _rope_tables(nlat, nctx):
    t = jnp.arange(nlat, dtype=jnp.int32)
    row = (t // GRID_W).astype(F32)
    col = (t % GRID_W).astype(F32)
    half = MLA_ROPE // 2
    inv = 1.0 / (ROPE_THETA ** (jnp.arange(0, half, 2, dtype=F32) / half))
    ar = row[:, None] * inv[None, :]
    ac = col[:, None] * inv[None, :]
    one = jnp.ones((nlat, MLA_NOPE), F32)
    tail = jnp.ones((nlat, LANES - MLA_QK), F32)
    cos_t = jnp.concatenate([one, jnp.cos(ar), jnp.cos(ar), jnp.cos(ac), jnp.cos(ac), tail], axis=1)
    sin_t = jnp.concatenate([0 * one, -jnp.sin(ar), jnp.sin(ar), -jnp.sin(ac), jnp.sin(ac), 0 * tail], axis=1)
    cos_t = jnp.concatenate([cos_t, jnp.ones((nctx, LANES), F32)], axis=0)
    sin_t = jnp.concatenate([sin_t, jnp.zeros((nctx, LANES), F32)], axis=0)
    return cos_t, sin_t


def _na_bias_tables(rpb):
    depth, h = rpb.shape[:2]
    wr, wc = NA_WIN_ROWS, NA_WIN_COLS
    col = np.arange(GRID_W)
    col_start = np.clip(col - wc // 2, 0, GRID_W - wc)
    in_win = (col[None, :] >= col_start[:, None]) & (col[None, :] < col_start[:, None] + wc)
    idx_c = np.clip(col[None, :] - col[:, None], -(wc - 1), wc - 1) + wc - 1
    idx_r = np.arange(wr)[None, :] - np.arange(wr)[:, None] + wr - 1
    oh_r = (idx_r[:, :, None] == np.arange(2 * wr - 1)).astype(np.float32)
    oh_c = (idx_c[:, :, None] == np.arange(2 * wc - 1)).astype(np.float32)
    hp = lax.Precision.HIGHEST
    rows = jnp.einsum('lhrc,djr->lhdjc', rpb, oh_r, precision=hp)
    b = jnp.einsum('lhdjc,qkc->lhdqjk', rows, oh_c, precision=hp)
    b = jnp.where(in_win[:, None, :], b * LOG2E, NEG_INF)
    return b.reshape(depth * h, wr, GRID_W, wr * GRID_W).astype(F32)


def _s5_mats(lam_re, lam_im, log_dt, b_re, b_im, c_re, c_im):
    hp = lax.Precision.HIGHEST
    t = SSM_CHUNK
    g, p = lam_re.shape[1], lam_re.shape[2]
    gi = b_re.shape[-1]
    gb = LANES // gi
    nblk = g // gb
    dt = jnp.exp(log_dt.astype(F32))[..., None]
    lr, li = lam_re.astype(F32), lam_im.astype(F32)
    mag = jnp.exp(lr * dt)
    ab_re, ab_im = mag * jnp.cos(li * dt), mag * jnp.sin(li * dt)
    nr = ab_re - 1.0
    den = lr * lr + li * li
    fr = (nr * lr + ab_im * li) / den
    fi = (ab_im * lr - nr * li) / den
    bb_re = fr[..., None] * b_re - fi[..., None] * b_im
    bb_im = fr[..., None] * b_im + fi[..., None] * b_re
    j = jnp.arange(t + 1, dtype=F32)[None, None, :, None]
    pmag = jnp.exp(lr[:, :, None, :] * dt[:, :, None, :] * j)
    ang = li[:, :, None, :] * dt[:, :, None, :] * j
    ap_re, ap_im = pmag * jnp.cos(ang), pmag * jnp.sin(ang)

    ca_re = c_re[:, :, None] * ap_re[:, :, :, None, :] - c_im[:, :, None] * ap_im[:, :, :, None, :]
    ca_im = c_re[:, :, None] * ap_im[:, :, :, None, :] + c_im[:, :, None] * ap_re[:, :, :, None, :]
    kmat = (jnp.einsum('dgjip,dgpk->dgjik', ca_re, bb_re, precision=hp)
            - jnp.einsum('dgjip,dgpk->dgjik', ca_im, bb_im, precision=hp))
    tt = np.arange(t)
    lag = tt[None, :] - tt[:, None]
    sel_f = (lag[:, :, None] == np.arange(t + 1)).astype(np.float32)
    sel_b = (-lag[:, :, None] == np.arange(t + 1)).astype(np.float32)
    toep = (jnp.einsum('ktj,gjab->gkbta', sel_f, kmat[0], precision=hp)
            + jnp.einsum('ktj,gjab->gkbta', sel_b, kmat[1], precision=hp))

    def end_map(d, powers):
        are, aim = ap_re[d][:, powers], ap_im[d][:, powers]
        bre, bim = jnp.transpose(bb_re[d], (0, 2, 1)), jnp.transpose(bb_im[d], (0, 2, 1))
        mre = are[:, :, None, :] * bre[:, None] - aim[:, :, None, :] * bim[:, None]
        mim = are[:, :, None, :] * bim[:, None] + aim[:, :, None, :] * bre[:, None]
        return mre, mim

    ef_re, ef_im = end_map(0, t - 1 - tt)
    eb_re, eb_im = end_map(1, tt)
    mend = jnp.stack([ef_re, ef_im, eb_re, eb_im], axis=3)

    def out_map(d, powers):
        cre = jnp.transpose(ca_re[d][:, powers], (0, 3, 1, 2))
        cim = jnp.transpose(ca_im[d][:, powers], (0, 3, 1, 2))
        return cre, -cim

    of_re, of_im = out_map(0, tt + 1)
    ob_re, ob_im = out_map(1, t - tt)
    mout = jnp.stack([of_re, of_im, ob_re, ob_im], axis=1)

    eye = np.eye(gb, dtype=np.float32)[None, :, None, None, None, :, None]
    wy = toep.reshape(nblk, gb, t, gi, t, 1, gi) * eye
    wy = jnp.transpose(wy, (0, 2, 1, 3, 4, 5, 6)).reshape(nblk, t * gb * gi, t * gb * gi)
    we = mend.reshape(nblk, gb, t, gi, 4, 1, p) * eye
    we = jnp.transpose(we, (0, 2, 1, 3, 4, 5, 6)).reshape(nblk, t * gb * gi, 4 * gb * p)
    mo = mout.reshape(nblk, gb, 4, p, t, 1, gi) * eye
    mo = jnp.transpose(mo, (0, 2, 1, 3, 4, 5, 6)).reshape(nblk, 4 * gb * p, t * gb * gi)

    a16 = jnp.stack([ap_re[0][:, t], ap_im[0][:, t], ap_re[1][:, t], ap_im[1][:, t]], axis=1)
    a_blk = jnp.transpose(a16.reshape(nblk, gb, 4, p), (0, 2, 1, 3)).reshape(nblk, 4, gb * p)
    return we.astype(BF16), wy.astype(BF16), mo.astype(BF16), a_blk


def kernel(x, c, ctx, c_ctx, w_mod, b_mod, w_in, ssm_lam_re, ssm_lam_im, ssm_log_dt, ssm_b_re, ssm_b_im,
           ssm_c_re, ssm_c_im, ssm_d, ssm_w_glu, ssm_b_glu, na_rpb, mla_q_norm, mla_w_uq, mla_kv_norm,
           mla_w_ukv, w_branch_a, w_branch_b, w_branch_c, w_out, ln_g, ln_b):
    b, nlat, d = x.shape
    nctx = ctx.shape[1]
    ltot = nlat + nctx
    depth = w_mod.shape[0]
    assert nlat % TOKEN_TILE == 0 and nctx % TOKEN_TILE == 0 and nlat % nctx == 0
    assert nlat % (NA_ROW_BLOCK * GRID_W) == 0 and b + 1 <= 8
    assert nlat % (S5_SUB * SSM_CHUNK) == 0 and nctx % (S5_SUB * SSM_CHUNK) == 0
    n_lat_tiles = nlat // TOKEN_TILE
    alpha = (2 * depth) ** 0.25
    fa_tq = FA_TQ if nlat % FA_TQ == 0 else nctx
    fa_tk = FA_TK if ltot % FA_TK == 0 else nctx

    cc = jnp.concatenate([c, c_ctx[None], jnp.zeros((8 - b - 1, d), F32)], axis=0)
    mod_all = _mod_call(cc, w_mod, b_mod)
    mod_lat = mod_all[:, :b].reshape(depth, b, 1, 3, d)
    mod_ctx = jnp.broadcast_to(mod_all[:, b].reshape(depth, 1, 1, 3, d), (depth, b, 1, 3, d))
    modt = jnp.concatenate([mod_lat, mod_ctx], axis=2)

    xcat = jnp.concatenate([x, ctx], axis=1)
    cos_t, sin_t = _rope_tables(nlat, nctx)

    w_pad = _prep_w_in(w_in)
    we, wy, mo, a_blk = jax.vmap(_s5_mats)(ssm_lam_re, ssm_lam_im, ssm_log_dt, ssm_b_re, ssm_b_im,
                                            ssm_c_re, ssm_c_im)
    we, wy, mo, a_blk = (v.reshape((-1,) + v.shape[2:]) for v in (we, wy, mo, a_blk))
    bias_all = _na_bias_tables(na_rpb)
    wq, wqs, wk, wks, wv = jax.vmap(_prep_mla)(mla_w_uq, mla_w_ukv)
    ones = _mla_ones_row()
    bf = lambda w: w.astype(BF16)
    row = lambda v: v.reshape(depth, 1, v.shape[-1])
    wglu, wa, wb, wc, wo = bf(ssm_w_glu), bf(w_branch_a), bf(w_branch_b), bf(w_branch_c), bf(w_out)
    bglu, dskip, lng, lnb = row(ssm_b_glu), row(ssm_d), row(ln_g), row(ln_b)
    gq, gkv = row(mla_q_norm), row(mla_kv_norm)

    for l in range(depth):
        n_out_tiles = n_lat_tiles if l == depth - 1 else ltot // TOKEN_TILE
        u, z, na, mc, g = _in_call(xcat, modt, w_pad, n_lat_tiles, l)
        y = _s5_call(u, we, wy, mo, a_blk, nlat, nctx, l)
        yn = jnp.concatenate([_na_call(na, bias_all, nlat, nctx, l), _na_ctx_call(na, nlat, nctx)], axis=1)
        q, k, vt = _mlap_call(mc, cos_t, sin_t, gq, gkv, wq, wqs, wk, wks, wv, ones, l)
        ym = jnp.concatenate([_fa_call(q, k, vt, nlat, 0, ltot, 0, fa_tq, fa_tk),
                              _fa_call(q, k, vt, nctx, nlat, nctx, nlat, nctx, nctx)], axis=1)
        xcat = _merge_call(y, u, z, yn, ym, g, xcat, modt, wglu, bglu, dskip, wa, wb, wc, wo, lng, lnb,
                           n_lat_tiles, alpha, l, n_out_tiles)
    return xcat
```

```python
import functools
import math

import jax
import jax.numpy as jnp
import numpy as np
from jax import lax
from jax.experimental import pallas as pl
from jax.experimental.pallas import tpu as pltpu

F32 = jnp.float32
BF16 = jnp.bfloat16

GRID_W = 64
SSM_GROUP = 16
SSM_STATE = 64
SSM_CHUNK = 16
NA_HEADS = 8
NA_HEAD_DIM = 64
NA_WIN_ROWS = 8
NA_WIN_COLS = 16
MLA_HEADS = 8
MLA_Q_RANK = 256
MLA_KV_RANK = 128
MLA_NOPE = 64
MLA_ROPE = 32
MLA_V = 64
MLA_QK = MLA_NOPE + MLA_ROPE
ROPE_THETA = 10000.0
LN_EPS = 1e-5
RMS_EPS = 1e-6
NEG_INF = -1e30
LOG2E = math.log2(math.e)
LANES = 128
HALF = 64

TOKEN_TILE = 256
NA_ROW_BLOCK = 8
FA_TQ = 1024
FA_TK = 2816
VMEM_LIMIT = 56 * 1024 * 1024


def _sigmoid(x):
    return 1.0 / (1.0 + jnp.exp(-x))


def _dot(a, b):
    return jnp.dot(a, b, preferred_element_type=F32)


def _dot_nt(a, b):
    return lax.dot_general(a, b, (((1,), (1,)), ((), ())), preferred_element_type=F32)


def _params(*sem):
    return pltpu.CompilerParams(dimension_semantics=sem, vmem_limit_bytes=VMEM_LIMIT)


def _layer_spec(arr, layer, grid_rank):
    zeros = (0,) * (arr.ndim - 1)
    if grid_rank == 2:
        imap = lambda a, b: (layer,) + zeros
    else:
        imap = lambda a, b, c: (layer,) + zeros
    return pl.BlockSpec((None,) + arr.shape[1:], imap)


def _mod_kernel(cc_ref, w_ref, b_ref, o_ref):
    cc = cc_ref[...]
    s = (cc * _sigmoid(cc)).astype(BF16)
    o_ref[0] = _dot(s, w_ref[0].astype(BF16)) + b_ref[0]


def _mod_call(cc, w_mod, b_mod):
    depth, d, n = w_mod.shape
    tn = 1024
    return pl.pallas_call(
        _mod_kernel,
        out_shape=jax.ShapeDtypeStruct((depth, 8, n), F32),
        grid=(depth, n // tn),
        in_specs=[pl.BlockSpec((8, d), lambda l, j: (0, 0)),
                  pl.BlockSpec((1, d, tn), lambda l, j: (l, 0, j)),
                  pl.BlockSpec((1, 1, tn), lambda l, j: (l, 0, j))],
        out_specs=pl.BlockSpec((1, 8, tn), lambda l, j: (l, 0, j)),
        compiler_params=_params("parallel", "parallel"),
        name="adaln_mod",
    )(cc, w_mod, b_mod.reshape(depth, 1, n))


IN_WIDTH_PADDED = 7168
OUT_COLS = (512, 1536, 1536, 512, 3072)


def _in_kernel(x_ref, mod_ref, w_ref, o_u, o_z, o_na, o_mc, o_g):
    x = x_ref[0]
    mu = jnp.mean(x, axis=-1, keepdims=True)
    xc = x - mu
    var = jnp.mean(xc * xc, axis=-1, keepdims=True)
    mod = mod_ref[0, 0]
    h = (xc * lax.rsqrt(var + LN_EPS) * (1.0 + mod[1:2]) + mod[0:1]).astype(BF16)

    def mm(lo, hi):
        return _dot(h, w_ref[:, lo:hi])

    def silu(lo, hi):
        z = mm(lo, hi)
        return (z * _sigmoid(z)).astype(BF16)

    o_u[0] = mm(0, 512)
    o_z[0, :, 0:512] = silu(512, 1024)
    o_na[0] = mm(1024, 2560).astype(BF16)
    o_z[0, :, 512:1024] = silu(2560, 3072)
    o_mc[0] = mm(3072, 3584).astype(BF16)
    o_z[0, :, 1024:1536] = silu(3584, 4096)
    o_g[0] = _sigmoid(mm(4096, 7168)).astype(BF16)


def _in_call(xcat, modt, w_pad, n_lat_tiles, layer):
    b, ltot, d = xcat.shape
    tm = TOKEN_TILE
    tok = lambda w: pl.BlockSpec((1, tm, w), lambda bi, i: (bi, i, 0))
    dts = (F32, BF16, BF16, BF16, BF16)
    return pl.pallas_call(
        _in_kernel,
        out_shape=tuple(jax.ShapeDtypeStruct((b, ltot, w), dt) for w, dt in zip(OUT_COLS, dts)),
        grid=(b, ltot // tm),
        in_specs=[tok(d),
                  pl.BlockSpec((None, 1, 1, 3, d), lambda bi, i: (layer, bi, i // n_lat_tiles, 0, 0)),
                  pl.BlockSpec((None,) + w_pad.shape[1:], lambda bi, i: (layer, 0, 0),
                               pipeline_mode=pl.Buffered(1))],
        out_specs=tuple(tok(w) for w in OUT_COLS),
        compiler_params=_params("parallel", "parallel"),
        name="ln_in_proj",
    )(xcat, modt, w_pad)


S5_SUB = 8
S5_GB = LANES // SSM_GROUP
S5_STATE_COLS = S5_GB * SSM_STATE
S5_EXPAND_CHUNK = 512


def _s5_gather_chunks(u_ref, nch):
    cols = [u_ref[0, pl.ds(t, nch, stride=SSM_CHUNK), :].astype(BF16) for t in range(SSM_CHUNK)]
    return jnp.concatenate(cols, axis=1)


def _s5_expand(c_ref, rep_ref, w_scr, row_unit, col_unit):
    comp = c_ref[0]
    nrow, ncol = w_scr.shape
    ch = S5_EXPAND_CHUNK
    rg = (lax.broadcasted_iota(jnp.int32, (nrow, ch), 0) // row_unit) % S5_GB
    for j in range(ncol // ch):
        cg = ((lax.broadcasted_iota(jnp.int32, (nrow, ch), 1) + j * ch) // col_unit) % S5_GB
        full = _dot(comp, rep_ref[:, j * ch:(j + 1) * ch])
        w_scr[:, j * ch:(j + 1) * ch] = jnp.where(rg == cg, full, 0.0).astype(BF16)


def _s5_state_kernel(u_ref, wc_ref, rep_ref, a_ref, s_ref, we_scr, e_scr, s_scr, *, nlat, nctx):
    ntot = nlat + nctx
    w = S5_STATE_COLS

    @pl.when(pl.program_id(1) == 0)
    def _():
        _s5_expand(wc_ref, rep_ref, we_scr, SSM_GROUP, SSM_STATE)

    e_scr[...] = _dot(_s5_gather_chunks(u_ref, ntot), we_scr[...])
    a = a_ref[0]
    arf, aif, arb, aib = a[0:1], a[1:2], a[2:3], a[3:4]
    ntile, nctx_t, nlat_t = ntot // S5_SUB, nctx // S5_SUB, nlat // S5_SUB

    def sweep(er, ei, sr, si, ar, ai, order):
        rows_r, rows_i = [None] * S5_SUB, [None] * S5_SUB
        for k in order:
            rows_r[k], rows_i[k] = sr, si
            sr, si = ar * sr - ai * si + er[k:k + 1], ar * si + ai * sr + ei[k:k + 1]
        return jnp.concatenate(rows_r, axis=0), jnp.concatenate(rows_i, axis=0), sr, si

    def body(i, carry):
        srf, sif, srb, sib = carry
        rf = pl.multiple_of(jnp.where(i < nctx_t, nlat_t + i, i - nctx_t) * S5_SUB, S5_SUB)
        rb = pl.multiple_of((ntile - 1 - i) * S5_SUB, S5_SUB)
        pfr, pfi, srf, sif = sweep(e_scr[pl.ds(rf, S5_SUB), 0:w], e_scr[pl.ds(rf, S5_SUB), w:2 * w],
                                   srf, sif, arf, aif, range(S5_SUB))
        pbr, pbi, srb, sib = sweep(e_scr[pl.ds(rb, S5_SUB), 2 * w:3 * w], e_scr[pl.ds(rb, S5_SUB), 3 * w:4 * w],
                                   srb, sib, arb, aib, range(S5_SUB - 1, -1, -1))
        s_scr[pl.ds(rf, S5_SUB), 0:w] = pfr
        s_scr[pl.ds(rf, S5_SUB), w:2 * w] = pfi
        s_scr[pl.ds(rb, S5_SUB), 2 * w:3 * w] = pbr
        s_scr[pl.ds(rb, S5_SUB), 3 * w:4 * w] = pbi
        return srf, sif, srb, sib

    zero = jnp.zeros((1, w), F32)
    lax.fori_loop(0, ntile, body, (zero, zero, zero, zero))
    s_ref[0, 0] = s_scr[...].astype(BF16)


def _s5_out_kernel(u_ref, s_ref, tc_ref, mc_ref, rep_ref, y_ref, wy_scr, mo_scr, *, nch):
    @pl.when(pl.program_id(1) == 0)
    def _():
        _s5_expand(tc_ref, rep_ref, wy_scr, SSM_GROUP, SSM_GROUP)
        _s5_expand(mc_ref, rep_ref, mo_scr, SSM_STATE, SSM_GROUP)

    lhs = _s5_gather_chunks(u_ref, nch)
    s = s_ref[0, 0]
    half = SSM_CHUNK // 2
    for hf in range(2):
        cols = slice(hf * half * LANES, (hf + 1) * half * LANES)
        y = _dot(lhs, wy_scr[:, cols]) + _dot(s, mo_scr[:, cols])
        for t in range(half):
            y_ref[0, pl.ds(hf * half + t, nch, stride=SSM_CHUNK), :] = y[:, t * LANES:(t + 1) * LANES]


def _s5_call(u, wc, tc, mc, a, rep_state, rep_out, nlat, nctx, layer):
    b, ltot, width = u.shape
    nblk = width // LANES
    nch = ltot // SSM_CHUNK
    kdim = SSM_CHUNK * LANES
    scols = 4 * S5_STATE_COLS
    ublk = pl.BlockSpec((1, ltot, LANES), lambda j, bi: (bi, 0, j))
    sblk = pl.BlockSpec((1, 1, nch, scols), lambda j, bi: (bi, j, 0, 0))
    cblk = lambda c: pl.BlockSpec((1,) + c.shape[1:], lambda j, bi: (layer * nblk + j, 0, 0))
    rblk = lambda r: pl.BlockSpec(r.shape, lambda j, bi: (0, 0))
    sem = ("parallel", "arbitrary")
    states = pl.pallas_call(
        functools.partial(_s5_state_kernel, nlat=nlat // SSM_CHUNK, nctx=nctx // SSM_CHUNK),
        out_shape=jax.ShapeDtypeStruct((b, nblk, nch, scols), BF16),
        grid=(nblk, b),
        in_specs=[ublk, cblk(wc), rblk(rep_state), cblk(a)],
        out_specs=sblk,
        scratch_shapes=[pltpu.VMEM((kdim, scols), BF16), pltpu.VMEM((nch, scols), F32),
                        pltpu.VMEM((nch, scols), F32)],
        compiler_params=_params(*sem),
        name="s5_states",
    )(u, wc, rep_state, a)
    return pl.pallas_call(
        functools.partial(_s5_out_kernel, nch=nch),
        out_shape=jax.ShapeDtypeStruct((b, ltot, width), F32),
        grid=(nblk, b),
        in_specs=[ublk, sblk, cblk(tc), cblk(mc), rblk(rep_out)],
        out_specs=ublk,
        scratch_shapes=[pltpu.VMEM((kdim, kdim), BF16), pltpu.VMEM((scols, kdim), BF16)],
        compiler_params=_params(*sem),
        name="s5_outputs",
    )(u, states, tc, mc, rep_out)


NA_UNION = NA_ROW_BLOCK + NA_WIN_ROWS - 1


def _na_kernel(q_ref, k_ref, v_ref, pt_ref, o_ref, bias_scr, *, nrows, nlat, nctx):
    i = pl.program_id(2)
    nblk = nrows // NA_ROW_BLOCK
    r0 = i * NA_ROW_BLOCK
    half_win = NA_WIN_ROWS // 2
    ustart = jnp.clip(r0 - half_win, 0, nrows - NA_UNION)
    nkl = NA_UNION * GRID_W

    @pl.when((i <= 1) | (i == nblk - 1))
    def _():
        lane = lax.broadcasted_iota(jnp.int32, (1, LANES), 1)
        for rp in range(NA_ROW_BLOCK // 2):
            ra = r0 + 2 * rp
            sa = jnp.clip(ra - half_win, 0, nrows - NA_WIN_ROWS)
            sb = jnp.clip(ra + 1 - half_win, 0, nrows - NA_WIN_ROWS)
            for u in range(NA_UNION):
                kr = ustart + u
                pen_a = jnp.where((kr >= sa) & (kr < sa + NA_WIN_ROWS), 0.0, NEG_INF)
                pen_b = jnp.where((kr >= sb) & (kr < sb + NA_WIN_ROWS), 0.0, NEG_INF)
                pen = jnp.where(lane < HALF, pen_a, pen_b)
                d = jnp.clip(kr - ra + NA_WIN_ROWS - 1, 1, 2 * NA_WIN_ROWS - 2)
                for hh in range(2):
                    bias_scr[hh, u * GRID_W:(u + 1) * GRID_W, rp * LANES:(rp + 1) * LANES] = pt_ref[hh, d] + pen

    off = pl.multiple_of(ustart * GRID_W, GRID_W)
    kw = k_ref[0, pl.ds(off, nkl), :]
    kc = k_ref[0, nlat:nlat + nctx, :]
    vwt = v_ref[0, pl.ds(off, nkl), :].T
    vct = v_ref[0, nlat:nlat + nctx, :].T
    q = q_ref[0]
    lane = lax.broadcasted_iota(jnp.int32, q.shape, 1)
    outs = []
    for hh in range(2):
        qm = jnp.where(lane < HALF if hh == 0 else lane >= HALF, q, jnp.zeros_like(q))
        sw = _dot_nt(kw, qm) + bias_scr[hh]
        sc = _dot_nt(kc, qm)
        m = jnp.maximum(jnp.max(sw, axis=0, keepdims=True), jnp.max(sc, axis=0, keepdims=True))
        pw = jnp.exp2(sw - m)
        pc = jnp.exp2(sc - m)
        den = jnp.sum(pw, axis=0, keepdims=True) + jnp.sum(pc, axis=0, keepdims=True)
        o = _dot(vwt, pw.astype(BF16)) + _dot(vct, pc.astype(BF16))
        outs.append(o / den)
    ot = jnp.concatenate([outs[0][0:HALF], outs[1][HALF:LANES]], axis=0)
    o_ref[0] = ot.T.astype(BF16)


def _na_call(na, pt_all, nlat, nctx, layer):
    b, ltot, _ = na.shape
    nrows = nlat // GRID_W
    rb = NA_ROW_BLOCK
    npair = NA_HEADS // 2
    kern = functools.partial(_na_kernel, nrows=nrows, nlat=nlat, nctx=nctx)
    return pl.pallas_call(
        kern,
        out_shape=jax.ShapeDtypeStruct((b, nlat, NA_HEADS * NA_HEAD_DIM), BF16),
        grid=(b, npair, nrows // rb),
        in_specs=[pl.BlockSpec((1, rb * GRID_W, LANES), lambda bi, p, i: (bi, i, p)),
                  pl.BlockSpec((1, ltot, LANES), lambda bi, p, i: (bi, 0, npair + p)),
                  pl.BlockSpec((1, ltot, LANES), lambda bi, p, i: (bi, 0, 2 * npair + p)),
                  pl.BlockSpec((2, 2 * NA_WIN_ROWS - 1, GRID_W, LANES),
                               lambda bi, p, i: (layer * npair + p, 0, 0, 0))],
        out_specs=pl.BlockSpec((1, rb * GRID_W, LANES), lambda bi, p, i: (bi, i, p)),
        scratch_shapes=[pltpu.VMEM((2, NA_UNION * GRID_W, rb * GRID_W), F32)],
        compiler_params=_params("parallel", "parallel", "arbitrary"),
        name="na_latent",
    )(na, na, na, pt_all)


def _na_ctx_kernel(q_ref, k_ref, v_ref, o_ref):
    n = q_ref.shape[1]
    lane = lax.broadcasted_iota(jnp.int32, (n, LANES), 1)
    lo = lane < HALF
    hi = lane >= HALF
    q = q_ref[0]
    k = k_ref[0]
    v = v_ref[0]
    outs = []
    for hh in range(2):
        qm = jnp.where(lo if hh == 0 else hi, q, jnp.zeros_like(q))
        s = _dot_nt(qm, k)
        p = jnp.exp2(s - jnp.max(s, axis=-1, keepdims=True))
        outs.append(_dot(p.astype(BF16), v) / jnp.sum(p, axis=-1, keepdims=True))
    o_ref[0] = jnp.where(lo, outs[0], outs[1]).astype(BF16)


def _na_ctx_call(na, nlat, nctx):
    b = na.shape[0]
    npair = NA_HEADS // 2
    rblk = nlat // nctx
    return pl.pallas_call(
        _na_ctx_kernel,
        out_shape=jax.ShapeDtypeStruct((b, nctx, NA_HEADS * NA_HEAD_DIM), BF16),
        grid=(b, npair),
        in_specs=[pl.BlockSpec((1, nctx, LANES), lambda bi, p: (bi, rblk, p)),
                  pl.BlockSpec((1, nctx, LANES), lambda bi, p: (bi, rblk, npair + p)),
                  pl.BlockSpec((1, nctx, LANES), lambda bi, p: (bi, rblk, 2 * npair + p))],
        out_specs=pl.BlockSpec((1, nctx, LANES), lambda bi, p: (bi, 0, p)),
        compiler_params=_params("parallel", "parallel"),
        name="na_context",
    )(na, na, na)


def _mlap_kernel(mc_ref, cos_ref, sin_ref, gq_ref, gkv_ref, wq_ref, wqs_ref, wk_ref, wks_ref,
                 wv_ref, ones_ref, q_ref, k_ref, vt_ref):
    mc = mc_ref[0]
    cq = mc[:, 0:MLA_Q_RANK].astype(F32)
    ckv = mc[:, MLA_Q_RANK:MLA_Q_RANK + MLA_KV_RANK].astype(F32)
    kr = mc[:, MLA_Q_RANK + MLA_KV_RANK:]
    nq = (cq * lax.rsqrt(jnp.mean(cq * cq, axis=-1, keepdims=True) + RMS_EPS) * gq_ref[...]).astype(BF16)
    nkv = (ckv * lax.rsqrt(jnp.mean(ckv * ckv, axis=-1, keepdims=True) + RMS_EPS) * gkv_ref[...]).astype(BF16)
    cos = cos_ref[...]
    sin = sin_ref[...]
    q1 = _dot(nq, wq_ref[...])
    q2 = _dot(nq, wqs_ref[...])
    k1 = _dot(nkv, wk_ref[0:MLA_KV_RANK, :]) + _dot(kr, wk_ref[MLA_KV_RANK:, :])
    k2 = _dot(kr, wks_ref[...])
    v = _dot(nkv, wv_ref[...]) + ones_ref[...]
    for h in range(MLA_HEADS):
        sl = slice(h * LANES, (h + 1) * LANES)
        q_ref[0, h] = (q1[:, sl] * cos + q2[:, sl] * sin).astype(BF16)
        k_ref[0, h] = (k1[:, sl] * cos + k2[:, sl] * sin).astype(BF16)
        vt_ref[0, h] = v[:, sl].T.astype(BF16)


def _mlap_call(mc, cos_t, sin_t, gq, gkv, wq, wqs, wk, wks, wv, ones, layer):
    b, ltot, _ = mc.shape
    tm = TOKEN_TILE
    lay = lambda a: _layer_spec(a, layer, 2)
    hm = jax.ShapeDtypeStruct((b, MLA_HEADS, ltot, LANES), BF16)
    hspec = pl.BlockSpec((1, MLA_HEADS, tm, LANES), lambda bi, i: (bi, 0, i, 0))
    return pl.pallas_call(
        _mlap_kernel,
        out_shape=(hm, hm, jax.ShapeDtypeStruct((b, MLA_HEADS, LANES, ltot), BF16)),
        grid=(b, ltot // tm),
        in_specs=[pl.BlockSpec((1, tm, mc.shape[2]), lambda bi, i: (bi, i, 0)),
                  pl.BlockSpec((tm, LANES), lambda bi, i: (i, 0)),
                  pl.BlockSpec((tm, LANES), lambda bi, i: (i, 0)),
                  lay(gq), lay(gkv), lay(wq), lay(wqs), lay(wk), lay(wks), lay(wv),
                  pl.BlockSpec(ones.shape, lambda bi, i: (0, 0))],
        out_specs=(hspec, hspec, pl.BlockSpec((1, MLA_HEADS, LANES, tm), lambda bi, i: (bi, 0, 0, i))),
        compiler_params=_params("parallel", "parallel"),
        name="mla_proj",
    )(mc, cos_t, sin_t, gq, gkv, wq, wqs, wk, wks, wv, ones)


def _fa_kernel(q_ref, k_ref, vt_ref, o_ref, *, tq, tk, nk):
    qs = [q_ref[0, hh] for hh in range(2)]

    def body(c, carry):
        off = pl.multiple_of(c * tk, tk)
        new = []
        for hh in range(2):
            m, acc = carry[2 * hh], carry[2 * hh + 1]
            kc = k_ref[0, hh, pl.ds(off, tk), :]
            vc = vt_ref[0, hh, :, pl.ds(off, tk)]
            s = _dot_nt(kc, qs[hh])
            mn = jnp.maximum(m, jnp.max(s, axis=0, keepdims=True))
            p = jnp.exp2(s - mn).astype(BF16)
            new += [mn, acc * jnp.exp2(m - mn) + _dot(vc, p)]
        return tuple(new)

    m0 = jnp.full((1, tq), NEG_INF, F32)
    acc0 = jnp.zeros((LANES, tq), F32)
    res = lax.fori_loop(0, nk, body, (m0, acc0, m0, acc0))
    o0 = res[1][0:HALF] / res[1][HALF:HALF + 1]
    o1 = res[3][HALF:LANES] / res[3][0:1]
    o_ref[0] = jnp.concatenate([o0, o1], axis=0).T.astype(BF16)


def _fa_call(q, k, vt, nq_rows, q_row0, lk, k_row0, tq, tk):
    b, h, _, _ = q.shape
    npair = h // 2
    kern = functools.partial(_fa_kernel, tq=tq, tk=tk, nk=lk // tk)
    qb0 = q_row0 // tq
    kb0 = k_row0 // lk
    return pl.pallas_call(
        kern,
        out_shape=jax.ShapeDtypeStruct((b, nq_rows, h * HALF), BF16),
        grid=(b, npair, nq_rows // tq),
        in_specs=[pl.BlockSpec((1, 2, tq, LANES), lambda bi, p, i: (bi, p, qb0 + i, 0)),
                  pl.BlockSpec((1, 2, lk, LANES), lambda bi, p, i: (bi, p, kb0, 0)),
                  pl.BlockSpec((1, 2, LANES, lk), lambda bi, p, i: (bi, p, 0, kb0))],
        out_specs=pl.BlockSpec((1, tq, LANES), lambda bi, p, i: (bi, i, p)),
        compiler_params=_params("parallel", "parallel", "parallel"),
        name="mla_attn",
    )(q, k, vt)


def _gelu_tanh(x):
    return 0.5 * x * (1.0 + jnp.tanh(math.sqrt(2.0 / math.pi) * (x + 0.044715 * (x * x * x))))


def _merge_kernel(y_ref, u_ref, z_ref, yn_ref, ym_ref, g_ref, x_ref, mod_ref, wglu_ref, bglu_ref,
                  d_ref, wa_ref, wb_ref, wc_ref, wo_ref, lng_ref, lnb_ref, o_ref, *, alpha, d_model, width):
    ya = _gelu_tanh(y_ref[0] + d_ref[...] * u_ref[0])
    ya = ya * _sigmoid(_dot(ya.astype(BF16), wglu_ref[...]) + bglu_ref[...])
    z = z_ref[0]
    g = g_ref[0]
    ba = _dot((ya * z[:, 0:width].astype(F32)).astype(BF16), wa_ref[...])
    bn = _dot(yn_ref[0] * z[:, width:2 * width], wb_ref[...])
    bm = _dot(ym_ref[0] * z[:, 2 * width:3 * width], wc_ref[...])
    m = (g[:, 0:d_model].astype(F32) * ba + g[:, d_model:2 * d_model].astype(F32) * bn
         + g[:, 2 * d_model:3 * d_model].astype(F32) * bm)
    out = _dot(m.astype(BF16), wo_ref[...])
    t = alpha * x_ref[0] + mod_ref[0, 0][2:3] * out
    mu = jnp.mean(t, axis=-1, keepdims=True)
    tc = t - mu
    var = jnp.mean(tc * tc, axis=-1, keepdims=True)
    o_ref[0] = tc * lax.rsqrt(var + LN_EPS) * lng_ref[...] + lnb_ref[...]


def _merge_call(y, u, z, yn, ym, g, xcat, modt, wglu, bglu, dskip, wa, wb, wc, wo, lng, lnb,
                n_lat_tiles, alpha, layer, n_out_tiles):
    b, _, d = xcat.shape
    width = u.shape[2]
    tm = TOKEN_TILE
    tok = lambda a: pl.BlockSpec((1, tm, a.shape[2]), lambda bi, i: (bi, i, 0))
    lay = lambda a: _layer_spec(a, layer, 2)
    kern = functools.partial(_merge_kernel, alpha=alpha, d_model=d, width=width)
    return pl.pallas_call(
        kern,
        out_shape=jax.ShapeDtypeStruct((b, n_out_tiles * tm, d), F32),
        grid=(b, n_out_tiles),
        in_specs=[tok(y), tok(u), tok(z), tok(yn), tok(ym), tok(g), tok(xcat),
                  pl.BlockSpec((None, 1, 1, 3, d), lambda bi, i: (layer, bi, i // n_lat_tiles, 0, 0)),
                  lay(wglu), lay(bglu), lay(dskip), lay(wa), lay(wb), lay(wc), lay(wo), lay(lng), lay(lnb)],
        out_specs=pl.BlockSpec((1, tm, d), lambda bi, i: (bi, i, 0)),
        compiler_params=_params("parallel", "parallel"),
        name="merge_deepnorm",
    )(y, u, z, yn, ym, g, xcat, modt, wglu, bglu, dskip, wa, wb, wc, wo, lng, lnb)


IN_PAD_AT = 6 * 512 + MLA_Q_RANK + MLA_KV_RANK + MLA_ROPE


def _prep_w_in(w_in):
    pad = jnp.zeros(w_in.shape[:-1] + (LANES - MLA_ROPE,), w_in.dtype)
    scale = np.ones((IN_WIDTH_PADDED,), np.float32)
    scale[1024:1536] = NA_HEAD_DIM ** -0.5 * LOG2E
    w = jnp.concatenate([w_in[..., :IN_PAD_AT], pad, w_in[..., IN_PAD_AT:]], axis=-1)
    return (w * scale).astype(BF16)


_ROPE_SWAP = np.concatenate([np.arange(8, 16), np.arange(0, 8), np.arange(24, 32), np.arange(16, 24)])


def _prep_mla(w_uq, w_ukv):
    h = MLA_HEADS
    scale = MLA_QK ** -0.5 * LOG2E
    wq = w_uq.reshape(MLA_Q_RANK, h, MLA_QK) * scale
    zq = jnp.zeros((MLA_Q_RANK, h, LANES - MLA_QK), F32)
    nope, pe = wq[:, :, :MLA_NOPE], wq[:, :, MLA_NOPE:]
    wq_main = jnp.concatenate([nope, pe, zq], axis=2).reshape(MLA_Q_RANK, h * LANES)
    wq_swap = jnp.concatenate([jnp.zeros_like(nope), pe[:, :, _ROPE_SWAP], zq], axis=2).reshape(MLA_Q_RANK, h * LANES)

    wkv = w_ukv.reshape(MLA_KV_RANK, h, MLA_NOPE + MLA_V)
    knope, wv = wkv[:, :, :MLA_NOPE], wkv[:, :, MLA_NOPE:]
    zk = jnp.zeros((MLA_KV_RANK, h, LANES - MLA_NOPE), F32)
    wk_top = jnp.concatenate([knope, zk], axis=2).reshape(MLA_KV_RANK, h * LANES)
    eye = np.zeros((LANES, LANES), np.float32)
    eye[np.arange(MLA_ROPE), MLA_NOPE + np.arange(MLA_ROPE)] = 1.0
    eye_sw = np.zeros((LANES, LANES), np.float32)
    eye_sw[_ROPE_SWAP, MLA_NOPE + np.arange(MLA_ROPE)] = 1.0
    wk_bot = jnp.asarray(np.tile(eye, (1, h)))
    wk_swap = jnp.asarray(np.tile(eye_sw, (1, h)))
    wk_main = jnp.concatenate([wk_top, wk_bot], axis=0)

    zv = jnp.zeros_like(wv)
    even = jnp.concatenate([wv, zv], axis=2)
    odd = jnp.concatenate([zv, wv], axis=2)
    is_even = (np.arange(h) % 2 == 0)[None, :, None]
    wv_full = jnp.where(is_even, even, odd).reshape(MLA_KV_RANK, h * LANES)
    return (wq_main.astype(BF16), wq_swap.astype(BF16), wk_main.astype(BF16), wk_swap.astype(BF16),
            wv_full.astype(BF16))


def _mla_ones_row():
    ones = np.zeros((1, MLA_HEADS, LANES), np.float32)
    ones[0, 0::2, MLA_V] = 1.0
    ones[0, 1::2, 0] = 1.0
    return jnp.asarray(ones.reshape(1, MLA_HEADS * LANES))


def _rope_tables(nlat, nctx):
    t = jnp.arange(nlat, dtype=jnp.int32)
    row = (t // GRID_W).astype(F32)
    col = (t % GRID_W).astype(F32)
    half = MLA_ROPE // 2
    inv = 1.0 / (ROPE_THETA ** (jnp.arange(0, half, 2, dtype=F32) / half))
    ar = row[:, None] * inv[None, :]
    ac = col[:, None] * inv[None, :]
    one = jnp.ones((nlat, MLA_NOPE), F32)
    tail = jnp.ones((nlat, LANES - MLA_QK), F32)
    cos_t = jnp.concatenate([one, jnp.cos(ar), jnp.cos(ar), jnp.cos(ac), jnp.cos(ac), tail], axis=1)
    sin_t = jnp.concatenate([0 * one, -jnp.sin(ar), jnp.sin(ar), -jnp.sin(ac), jnp.sin(ac), 0 * tail], axis=1)
    cos_t = jnp.concatenate([cos_t, jnp.ones((nctx, LANES), F32)], axis=0)
    sin_t = jnp.concatenate([sin_t, jnp.zeros((nctx, LANES), F32)], axis=0)
    return cos_t, sin_t


def _na_bias_tables(rpb):
    depth, h = rpb.shape[:2]
    wc = NA_WIN_COLS
    col = np.arange(GRID_W)
    col_start = np.clip(col - wc // 2, 0, GRID_W - wc)
    in_win = (col[None, :] >= col_start[:, None]) & (col[None, :] < col_start[:, None] + wc)
    idx_c = np.clip(col[None, :] - col[:, None], -(wc - 1), wc - 1) + wc - 1
    oh_c = (idx_c.T[:, :, None] == np.arange(2 * wc - 1)).astype(np.float32)
    bt = jnp.einsum('lhdc,kqc->lhdkq', rpb, oh_c, precision=lax.Precision.HIGHEST)
    bt = jnp.where(in_win.T, bt * LOG2E, NEG_INF)
    pt = jnp.concatenate([bt, jnp.roll(bt, 1, axis=2)], axis=-1)
    return pt.reshape(depth * h, 2 * NA_WIN_ROWS - 1, GRID_W, 2 * GRID_W).astype(F32)


def _s5_mats(lam_re, lam_im, log_dt, b_re, b_im, c_re, c_im):
    hp = lax.Precision.HIGHEST
    t = SSM_CHUNK
    g, p = lam_re.shape[1], lam_re.shape[2]
    gi = b_re.shape[-1]
    gb = LANES // gi
    nblk = g // gb
    dt = jnp.exp(log_dt.astype(F32))[..., None]
    lr, li = lam_re.astype(F32), lam_im.astype(F32)
    mag = jnp.exp(lr * dt)
    ab_re, ab_im = mag * jnp.cos(li * dt), mag * jnp.sin(li * dt)
    nr = ab_re - 1.0
    den = lr * lr + li * li
    fr = (nr * lr + ab_im * li) / den
    fi = (ab_im * lr - nr * li) / den
    bb_re = fr[..., None] * b_re - fi[..., None] * b_im
    bb_im = fr[..., None] * b_im + fi[..., None] * b_re
    j = jnp.arange(t + 1, dtype=F32)[None, None, :, None]
    pmag = jnp.exp(lr[:, :, None, :] * dt[:, :, None, :] * j)
    ang = li[:, :, None, :] * dt[:, :, None, :] * j
    ap_re, ap_im = pmag * jnp.cos(ang), pmag * jnp.sin(ang)

    ca_re = c_re[:, :, None] * ap_re[:, :, :, None, :] - c_im[:, :, None] * ap_im[:, :, :, None, :]
    ca_im = c_re[:, :, None] * ap_im[:, :, :, None, :] + c_im[:, :, None] * ap_re[:, :, :, None, :]
    kmat = (jnp.einsum('dgjip,dgpk->dgjik', ca_re, bb_re, precision=hp)
            - jnp.einsum('dgjip,dgpk->dgjik', ca_im, bb_im, precision=hp))
    tt = np.arange(t)
    lag = tt[None, :] - tt[:, None]
    sel_f = (lag[:, :, None] == np.arange(t + 1)).astype(np.float32)
    sel_b = (-lag[:, :, None] == np.arange(t + 1)).astype(np.float32)
    toep = (jnp.einsum('ktj,gjab->gkbta', sel_f, kmat[0], precision=hp)
            + jnp.einsum('ktj,gjab->gkbta', sel_b, kmat[1], precision=hp))

    def end_map(d, powers):
        are, aim = ap_re[d][:, powers], ap_im[d][:, powers]
        bre, bim = jnp.transpose(bb_re[d], (0, 2, 1)), jnp.transpose(bb_im[d], (0, 2, 1))
        mre = are[:, :, None, :] * bre[:, None] - aim[:, :, None, :] * bim[:, None]
        mim = are[:, :, None, :] * bim[:, None] + aim[:, :, None, :] * bre[:, None]
        return mre, mim

    ef_re, ef_im = end_map(0, t - 1 - tt)
    eb_re, eb_im = end_map(1, tt)
    mend = jnp.stack([ef_re, ef_im, eb_re, eb_im], axis=3)

    def out_map(d, powers):
        cre = jnp.transpose(ca_re[d][:, powers], (0, 3, 1, 2))
        cim = jnp.transpose(ca_im[d][:, powers], (0, 3, 1, 2))
        return cre, -cim

    of_re, of_im = out_map(0, tt + 1)
    ob_re, ob_im = out_map(1, t - tt)
    mout = jnp.stack([of_re, of_im, ob_re, ob_im], axis=1)

    def compact(m, rows):
        m = m.reshape(nblk, gb, m.shape[1], m.shape[2], -1)
        return jnp.transpose(m, (0, 2, 1, 3, 4)).reshape(nblk, rows, m.shape[-1]).astype(BF16)

    wc = compact(mend, t * gb * gi)
    tc = compact(toep, t * gb * gi)
    mc = compact(mout, 4 * gb * p)

    a16 = jnp.stack([ap_re[0][:, t], ap_im[0][:, t], ap_re[1][:, t], ap_im[1][:, t]], axis=1)
    a_blk = jnp.transpose(a16.reshape(nblk, gb, 4, p), (0, 2, 1, 3)).reshape(nblk, 4, gb * p)
    return wc, tc, mc, a_blk


def _s5_replication():
    def rep(nx, ny):
        r = np.zeros((nx, ny, nx, S5_GB, ny), np.float32)
        ix, iy = np.meshgrid(np.arange(nx), np.arange(ny), indexing='ij')
        r[ix, iy, ix, :, iy] = 1.0
        return jnp.asarray(r.reshape(nx * ny, nx * S5_GB * ny), dtype=BF16)
    return rep(4, SSM_STATE), rep(SSM_CHUNK, SSM_GROUP)


def kernel(x, c, ctx, c_ctx, w_mod, b_mod, w_in, ssm_lam_re, ssm_lam_im, ssm_log_dt, ssm_b_re, ssm_b_im,
           ssm_c_re, ssm_c_im, ssm_d, ssm_w_glu, ssm_b_glu, na_rpb, mla_q_norm, mla_w_uq, mla_kv_norm,
           mla_w_ukv, w_branch_a, w_branch_b, w_branch_c, w_out, ln_g, ln_b):
    b, nlat, d = x.shape
    nctx = ctx.shape[1]
    ltot = nlat + nctx
    depth = w_mod.shape[0]
    assert nlat % TOKEN_TILE == 0 and nctx % TOKEN_TILE == 0 and nlat % nctx == 0
    assert nlat % (NA_ROW_BLOCK * GRID_W) == 0 and b + 1 <= 8
    assert nlat % (S5_SUB * SSM_CHUNK) == 0 and nctx % (S5_SUB * SSM_CHUNK) == 0
    n_lat_tiles = nlat // TOKEN_TILE
    alpha = (2 * depth) ** 0.25
    fa_tq = FA_TQ if nlat % FA_TQ == 0 else nctx
    fa_tk = FA_TK if ltot % FA_TK == 0 else nctx

    cc = jnp.concatenate([c, c_ctx[None], jnp.zeros((8 - b - 1, d), F32)], axis=0)
    mod_all = _mod_call(cc, w_mod, b_mod)
    mod_lat = mod_all[:, :b].reshape(depth, b, 1, 3, d)
    mod_ctx = jnp.broadcast_to(mod_all[:, b].reshape(depth, 1, 1, 3, d), (depth, b, 1, 3, d))
    modt = jnp.concatenate([mod_lat, mod_ctx], axis=2)

    xcat = jnp.concatenate([x, ctx], axis=1)
    cos_t, sin_t = _rope_tables(nlat, nctx)

    w_pad = _prep_w_in(w_in)
    s5w = jax.vmap(_s5_mats)(ssm_lam_re, ssm_lam_im, ssm_log_dt, ssm_b_re, ssm_b_im, ssm_c_re, ssm_c_im)
    s5_wc, s5_tc, s5_mc, s5_a = (v.reshape((-1,) + v.shape[2:]) for v in s5w)
    rep_state, rep_out = _s5_replication()
    bias_all = _na_bias_tables(na_rpb)
    wq, wqs, wk, wks, wv = jax.vmap(_prep_mla)(mla_w_uq, mla_w_ukv)
    ones = _mla_ones_row()
    bf = lambda w: w.astype(BF16)
    row = lambda v: v.reshape(depth, 1, v.shape[-1])
    wglu, wa, wb, wc, wo = bf(ssm_w_glu), bf(w_branch_a), bf(w_branch_b), bf(w_branch_c), bf(w_out)
    bglu, dskip, lng, lnb = row(ssm_b_glu), row(ssm_d), row(ln_g), row(ln_b)
    gq, gkv = row(mla_q_norm), row(mla_kv_norm)

    for l in range(depth):
        n_out_tiles = n_lat_tiles if l == depth - 1 else ltot // TOKEN_TILE
        u, z, na, mc, g = _in_call(xcat, modt, w_pad, n_lat_tiles, l)
        y = _s5_call(u, s5_wc, s5_tc, s5_mc, s5_a, rep_state, rep_out, nlat, nctx, l)
        yn = jnp.concatenate([_na_call(na, bias_all, nlat, nctx, l), _na_ctx_call(na, nlat, nctx)], axis=1)
        q, k, vt = _mlap_call(mc, cos_t, sin_t, gq, gkv, wq, wqs, wk, wks, wv, ones, l)
        ym = jnp.concatenate([_fa_call(q, k, vt, nlat, 0, ltot, 0, fa_tq, fa_tk),
                              _fa_call(q, k, vt, nctx, nlat, nctx, nlat, nctx, nctx)], axis=1)
        xcat = _merge_call(y, u, z, yn, ym, g, xcat, modt, wglu, bglu, dskip, wa, wb, wc, wo, lng, lnb,
                           n_lat_tiles, alpha, l, n_out_tiles)
    return xcat
```

```python
import functools
import math

import jax
import jax.numpy as jnp
import numpy as np
from jax import lax
from jax.experimental import pallas as pl
from jax.experimental.pallas import tpu as pltpu

F32 = jnp.float32
BF16 = jnp.bfloat16

GRID_W = 64
SSM_GROUP = 16
SSM_STATE = 64
SSM_CHUNK = 16
NA_HEADS = 8
NA_HEAD_DIM = 64
NA_WIN_ROWS = 8
NA_WIN_COLS = 16
MLA_HEADS = 8
MLA_Q_RANK = 256
MLA_KV_RANK = 128
MLA_NOPE = 64
MLA_ROPE = 32
MLA_V = 64
MLA_QK = MLA_NOPE + MLA_ROPE
ROPE_THETA = 10000.0
LN_EPS = 1e-5
RMS_EPS = 1e-6
NEG_INF = -1e30
LOG2E = math.log2(math.e)
LANES = 128
HALF = 64

TOKEN_TILE = 256
NA_ROW_BLOCK = 8
FA_TQ = 2048
FA_TK = 1408
VMEM_LIMIT = 56 * 1024 * 1024


def _sigmoid(x):
    return 1.0 / (1.0 + jnp.exp(-x))


def _dot(a, b):
    return jnp.dot(a, b, preferred_element_type=F32)


def _dot_nt(a, b):
    return lax.dot_general(a, b, (((1,), (1,)), ((), ())), preferred_element_type=F32)


def _params(*sem):
    return pltpu.CompilerParams(dimension_semantics=sem, vmem_limit_bytes=VMEM_LIMIT)


def _layer_spec(arr, layer, grid_rank):
    zeros = (0,) * (arr.ndim - 1)
    if grid_rank == 2:
        imap = lambda a, b: (layer,) + zeros
    else:
        imap = lambda a, b, c: (layer,) + zeros
    return pl.BlockSpec((None,) + arr.shape[1:], imap)


def _mod_kernel(cc_ref, w_ref, b_ref, o_ref):
    cc = cc_ref[...]
    s = (cc * _sigmoid(cc)).astype(BF16)
    o_ref[0] = _dot(s, w_ref[0].astype(BF16)) + b_ref[0]


def _mod_call(cc, w_mod, b_mod):
    depth, d, n = w_mod.shape
    tn = 1024
    return pl.pallas_call(
        _mod_kernel,
        out_shape=jax.ShapeDtypeStruct((depth, 8, n), F32),
        grid=(depth, n // tn),
        in_specs=[pl.BlockSpec((8, d), lambda l, j: (0, 0)),
                  pl.BlockSpec((1, d, tn), lambda l, j: (l, 0, j)),
                  pl.BlockSpec((1, 1, tn), lambda l, j: (l, 0, j))],
        out_specs=pl.BlockSpec((1, 8, tn), lambda l, j: (l, 0, j)),
        compiler_params=_params("parallel", "parallel"),
        name="adaln_mod",
    )(cc, w_mod, b_mod.reshape(depth, 1, n))


IN_WIDTH_PADDED = 7168
OUT_COLS = (512, 1536, 1536, 512, 3072)


def _two_source_specs(width, n_lat_tiles, ctx_tile0):
    tm = TOKEN_TILE
    lat = pl.BlockSpec((1, tm, width), lambda bi, i: (bi, jnp.minimum(i, n_lat_tiles - 1), 0))
    ctx = pl.BlockSpec((1, tm, width), lambda bi, i: (bi, ctx_tile0 + jnp.maximum(i - n_lat_tiles, 0), 0))
    return [lat, ctx]


def _pick_source(lat_ref, ctx_ref, n_lat_tiles):
    return jnp.where(pl.program_id(1) >= n_lat_tiles, ctx_ref[0], lat_ref[0])


def _in_kernel(xl_ref, xc_ref, mod_ref, w_ref, o_u, o_z, o_na, o_mc, o_g, *, n_lat_tiles):
    x = _pick_source(xl_ref, xc_ref, n_lat_tiles)
    mu = jnp.mean(x, axis=-1, keepdims=True)
    xc = x - mu
    var = jnp.mean(xc * xc, axis=-1, keepdims=True)
    mod = mod_ref[0, 0]
    h = (xc * lax.rsqrt(var + LN_EPS) * (1.0 + mod[1:2]) + mod[0:1]).astype(BF16)

    def mm(lo, hi):
        return _dot(h, w_ref[:, lo:hi])

    def silu(lo, hi):
        z = mm(lo, hi)
        return (z * _sigmoid(z)).astype(BF16)

    o_u[0] = mm(0, 512)
    o_z[0, :, 0:512] = silu(512, 1024)
    o_na[0] = mm(1024, 2560).astype(BF16)
    o_z[0, :, 512:1024] = silu(2560, 3072)
    o_mc[0] = mm(3072, 3584).astype(BF16)
    o_z[0, :, 1024:1536] = silu(3584, 4096)
    o_g[0] = _sigmoid(mm(4096, 7168)).astype(BF16)


def _in_call(x_lat, x_ctx, ctx_tile0, ltot, modt, w_pad, n_lat_tiles, layer):
    b, _, d = x_lat.shape
    tm = TOKEN_TILE
    tok = lambda w: pl.BlockSpec((1, tm, w), lambda bi, i: (bi, i, 0))
    dts = (F32, BF16, BF16, BF16, BF16)
    return pl.pallas_call(
        functools.partial(_in_kernel, n_lat_tiles=n_lat_tiles),
        out_shape=tuple(jax.ShapeDtypeStruct((b, ltot, w), dt) for w, dt in zip(OUT_COLS, dts)),
        grid=(b, ltot // tm),
        in_specs=_two_source_specs(d, n_lat_tiles, ctx_tile0) + [
            pl.BlockSpec((None, 1, 1, 3, d), lambda bi, i: (layer, bi, i // n_lat_tiles, 0, 0)),
            pl.BlockSpec((None,) + w_pad.shape[1:], lambda bi, i: (layer, 0, 0), pipeline_mode=pl.Buffered(1))],
        out_specs=tuple(tok(w) for w in OUT_COLS),
        compiler_params=_params("parallel", "parallel"),
        name="ln_in_proj",
    )(x_lat, x_ctx, modt, w_pad)


S5_SUB = 8
S5_GB = LANES // SSM_GROUP
S5_STATE_COLS = S5_GB * SSM_STATE
S5_EXPAND_CHUNK = 512


def _s5_gather_chunks(u_ref, nch):
    cols = [u_ref[0, pl.ds(t, nch, stride=SSM_CHUNK), :].astype(BF16) for t in range(SSM_CHUNK)]
    return jnp.concatenate(cols, axis=1)


def _s5_expand(c_ref, rep_ref, w_scr, row_unit, col_unit):
    comp = c_ref[0]
    nrow, ncol = w_scr.shape
    ch = S5_EXPAND_CHUNK
    rg = (lax.broadcasted_iota(jnp.int32, (nrow, ch), 0) // row_unit) % S5_GB
    for j in range(ncol // ch):
        cg = ((lax.broadcasted_iota(jnp.int32, (nrow, ch), 1) + j * ch) // col_unit) % S5_GB
        full = _dot(comp, rep_ref[:, j * ch:(j + 1) * ch])
        w_scr[:, j * ch:(j + 1) * ch] = jnp.where(rg == cg, full, 0.0).astype(BF16)


def _s5_state_kernel(u_ref, wc_ref, rep_ref, a_ref, s_ref, we_scr, e_scr, s_scr, *, nlat, nctx):
    ntot = nlat + nctx
    w = S5_STATE_COLS

    @pl.when(pl.program_id(1) == 0)
    def _():
        _s5_expand(wc_ref, rep_ref, we_scr, SSM_GROUP, SSM_STATE)

    e_scr[...] = _dot(_s5_gather_chunks(u_ref, ntot), we_scr[...])
    a = a_ref[0]
    arf, aif, arb, aib = a[0:1], a[1:2], a[2:3], a[3:4]
    ntile, nctx_t, nlat_t = ntot // S5_SUB, nctx // S5_SUB, nlat // S5_SUB

    def sweep(er, ei, sr, si, ar, ai, order):
        rows_r, rows_i = [None] * S5_SUB, [None] * S5_SUB
        for k in order:
            rows_r[k], rows_i[k] = sr, si
            sr, si = ar * sr - ai * si + er[k:k + 1], ar * si + ai * sr + ei[k:k + 1]
        return jnp.concatenate(rows_r, axis=0), jnp.concatenate(rows_i, axis=0), sr, si

    def body(i, carry):
        srf, sif, srb, sib = carry
        rf = pl.multiple_of(jnp.where(i < nctx_t, nlat_t + i, i - nctx_t) * S5_SUB, S5_SUB)
        rb = pl.multiple_of((ntile - 1 - i) * S5_SUB, S5_SUB)
        pfr, pfi, srf, sif = sweep(e_scr[pl.ds(rf, S5_SUB), 0:w], e_scr[pl.ds(rf, S5_SUB), w:2 * w],
                                   srf, sif, arf, aif, range(S5_SUB))
        pbr, pbi, srb, sib = sweep(e_scr[pl.ds(rb, S5_SUB), 2 * w:3 * w], e_scr[pl.ds(rb, S5_SUB), 3 * w:4 * w],
                                   srb, sib, arb, aib, range(S5_SUB - 1, -1, -1))
        s_scr[pl.ds(rf, S5_SUB), 0:w] = pfr
        s_scr[pl.ds(rf, S5_SUB), w:2 * w] = pfi
        s_scr[pl.ds(rb, S5_SUB), 2 * w:3 * w] = pbr
        s_scr[pl.ds(rb, S5_SUB), 3 * w:4 * w] = pbi
        return srf, sif, srb, sib

    zero = jnp.zeros((1, w), F32)
    lax.fori_loop(0, ntile, body, (zero, zero, zero, zero))
    s_ref[0, 0] = s_scr[...].astype(BF16)


def _s5_out_kernel(u_ref, s_ref, tc_ref, mc_ref, rep_ref, y_ref, wy_scr, mo_scr, *, nch):
    @pl.when(pl.program_id(1) == 0)
    def _():
        _s5_expand(tc_ref, rep_ref, wy_scr, SSM_GROUP, SSM_GROUP)
        _s5_expand(mc_ref, rep_ref, mo_scr, SSM_STATE, SSM_GROUP)

    lhs = _s5_gather_chunks(u_ref, nch)
    s = s_ref[0, 0]
    half = SSM_CHUNK // 2
    for hf in range(2):
        cols = slice(hf * half * LANES, (hf + 1) * half * LANES)
        y = _dot(lhs, wy_scr[:, cols]) + _dot(s, mo_scr[:, cols])
        for t in range(half):
            y_ref[0, pl.ds(hf * half + t, nch, stride=SSM_CHUNK), :] = y[:, t * LANES:(t + 1) * LANES]


def _s5_call(u, wc, tc, mc, a, rep_state, rep_out, nlat, nctx, layer):
    b, ltot, width = u.shape
    nblk = width // LANES
    nch = ltot // SSM_CHUNK
    kdim = SSM_CHUNK * LANES
    scols = 4 * S5_STATE_COLS
    ublk = pl.BlockSpec((1, ltot, LANES), lambda j, bi: (bi, 0, j))
    sblk = pl.BlockSpec((1, 1, nch, scols), lambda j, bi: (bi, j, 0, 0))
    cblk = lambda c: pl.BlockSpec((1,) + c.shape[1:], lambda j, bi: (layer * nblk + j, 0, 0))
    rblk = lambda r: pl.BlockSpec(r.shape, lambda j, bi: (0, 0))
    sem = ("parallel", "arbitrary")
    states = pl.pallas_call(
        functools.partial(_s5_state_kernel, nlat=nlat // SSM_CHUNK, nctx=nctx // SSM_CHUNK),
        out_shape=jax.ShapeDtypeStruct((b, nblk, nch, scols), BF16),
        grid=(nblk, b),
        in_specs=[ublk, cblk(wc), rblk(rep_state), cblk(a)],
        out_specs=sblk,
        scratch_shapes=[pltpu.VMEM((kdim, scols), BF16), pltpu.VMEM((nch, scols), F32),
                        pltpu.VMEM((nch, scols), F32)],
        compiler_params=_params(*sem),
        name="s5_states",
    )(u, wc, rep_state, a)
    return pl.pallas_call(
        functools.partial(_s5_out_kernel, nch=nch),
        out_shape=jax.ShapeDtypeStruct((b, ltot, width), F32),
        grid=(nblk, b),
        in_specs=[ublk, sblk, cblk(tc), cblk(mc), rblk(rep_out)],
        out_specs=ublk,
        scratch_shapes=[pltpu.VMEM((kdim, kdim), BF16), pltpu.VMEM((scols, kdim), BF16)],
        compiler_params=_params(*sem),
        name="s5_outputs",
    )(u, states, tc, mc, rep_out)


NA_UNION = NA_ROW_BLOCK + NA_WIN_ROWS - 1


def _na_kernel(q_ref, k_ref, v_ref, pt_ref, o_ref, bias_scr, *, nrows, nlat, nctx):
    i = pl.program_id(2)
    nblk = nrows // NA_ROW_BLOCK
    r0 = i * NA_ROW_BLOCK
    half_win = NA_WIN_ROWS // 2
    ustart = jnp.clip(r0 - half_win, 0, nrows - NA_UNION)
    nkl = NA_UNION * GRID_W

    @pl.when((i <= 1) | (i == nblk - 1))
    def _():
        lane = lax.broadcasted_iota(jnp.int32, (1, LANES), 1)
        for rp in range(NA_ROW_BLOCK // 2):
            ra = r0 + 2 * rp
            sa = jnp.clip(ra - half_win, 0, nrows - NA_WIN_ROWS)
            sb = jnp.clip(ra + 1 - half_win, 0, nrows - NA_WIN_ROWS)
            for u in range(NA_UNION):
                kr = ustart + u
                pen_a = jnp.where((kr >= sa) & (kr < sa + NA_WIN_ROWS), 0.0, NEG_INF)
                pen_b = jnp.where((kr >= sb) & (kr < sb + NA_WIN_ROWS), 0.0, NEG_INF)
                pen = jnp.where(lane < HALF, pen_a, pen_b)
                d = jnp.clip(kr - ra + NA_WIN_ROWS - 1, 1, 2 * NA_WIN_ROWS - 2)
                for hh in range(2):
                    bias_scr[hh, u * GRID_W:(u + 1) * GRID_W, rp * LANES:(rp + 1) * LANES] = pt_ref[hh, d] + pen

    off = pl.multiple_of(ustart * GRID_W, GRID_W)
    kw = k_ref[0, pl.ds(off, nkl), :]
    kc = k_ref[0, nlat:nlat + nctx, :]
    vwt = v_ref[0, pl.ds(off, nkl), :].T
    vct = v_ref[0, nlat:nlat + nctx, :].T
    q = q_ref[0]
    lane = lax.broadcasted_iota(jnp.int32, q.shape, 1)
    outs = []
    for hh in range(2):
        qm = jnp.where(lane < HALF if hh == 0 else lane >= HALF, q, jnp.zeros_like(q))
        sw = _dot_nt(kw, qm) + bias_scr[hh]
        sc = _dot_nt(kc, qm)
        m = jnp.maximum(jnp.max(sw, axis=0, keepdims=True), jnp.max(sc, axis=0, keepdims=True))
        pw = jnp.exp2(sw - m)
        pc = jnp.exp2(sc - m)
        den = jnp.sum(pw, axis=0, keepdims=True) + jnp.sum(pc, axis=0, keepdims=True)
        o = _dot(vwt, pw.astype(BF16)) + _dot(vct, pc.astype(BF16))
        outs.append(o / den)
    ot = jnp.concatenate([outs[0][0:HALF], outs[1][HALF:LANES]], axis=0)
    o_ref[0] = ot.T.astype(BF16)


def _na_call(na, pt_all, nlat, nctx, layer):
    b, ltot, _ = na.shape
    nrows = nlat // GRID_W
    rb = NA_ROW_BLOCK
    npair = NA_HEADS // 2
    kern = functools.partial(_na_kernel, nrows=nrows, nlat=nlat, nctx=nctx)
    return pl.pallas_call(
        kern,
        out_shape=jax.ShapeDtypeStruct((b, nlat, NA_HEADS * NA_HEAD_DIM), BF16),
        grid=(b, npair, nrows // rb),
        in_specs=[pl.BlockSpec((1, rb * GRID_W, LANES), lambda bi, p, i: (bi, i, p)),
                  pl.BlockSpec((1, ltot, LANES), lambda bi, p, i: (bi, 0, npair + p)),
                  pl.BlockSpec((1, ltot, LANES), lambda bi, p, i: (bi, 0, 2 * npair + p)),
                  pl.BlockSpec((2, 2 * NA_WIN_ROWS - 1, GRID_W, LANES),
                               lambda bi, p, i: (layer * npair + p, 0, 0, 0))],
        out_specs=pl.BlockSpec((1, rb * GRID_W, LANES), lambda bi, p, i: (bi, i, p)),
        scratch_shapes=[pltpu.VMEM((2, NA_UNION * GRID_W, rb * GRID_W), F32)],
        compiler_params=_params("parallel", "parallel", "arbitrary"),
        name="na_latent",
    )(na, na, na, pt_all)


def _na_ctx_kernel(q_ref, k_ref, v_ref, o_ref):
    n = q_ref.shape[1]
    lane = lax.broadcasted_iota(jnp.int32, (n, LANES), 1)
    lo = lane < HALF
    hi = lane >= HALF
    q = q_ref[0]
    k = k_ref[0]
    v = v_ref[0]
    outs = []
    for hh in range(2):
        qm = jnp.where(lo if hh == 0 else hi, q, jnp.zeros_like(q))
        s = _dot_nt(qm, k)
        p = jnp.exp2(s - jnp.max(s, axis=-1, keepdims=True))
        outs.append(_dot(p.astype(BF16), v) / jnp.sum(p, axis=-1, keepdims=True))
    o_ref[0] = jnp.where(lo, outs[0], outs[1]).astype(BF16)


def _na_ctx_call(na, nlat, nctx):
    b = na.shape[0]
    npair = NA_HEADS // 2
    rblk = nlat // nctx
    return pl.pallas_call(
        _na_ctx_kernel,
        out_shape=jax.ShapeDtypeStruct((b, nctx, NA_HEADS * NA_HEAD_DIM), BF16),
        grid=(b, npair),
        in_specs=[pl.BlockSpec((1, nctx, LANES), lambda bi, p: (bi, rblk, p)),
                  pl.BlockSpec((1, nctx, LANES), lambda bi, p: (bi, rblk, npair + p)),
                  pl.BlockSpec((1, nctx, LANES), lambda bi, p: (bi, rblk, 2 * npair + p))],
        out_specs=pl.BlockSpec((1, nctx, LANES), lambda bi, p: (bi, 0, p)),
        compiler_params=_params("parallel", "parallel"),
        name="na_context",
    )(na, na, na)


def _mlap_kernel(mc_ref, cos_ref, sin_ref, gq_ref, gkv_ref, wq_ref, wqs_ref, wk_ref, wks_ref,
                 wv_ref, ones_ref, q_ref, k_ref, vt_ref):
    mc = mc_ref[0]
    cq = mc[:, 0:MLA_Q_RANK].astype(F32)
    ckv = mc[:, MLA_Q_RANK:MLA_Q_RANK + MLA_KV_RANK].astype(F32)
    kr = mc[:, MLA_Q_RANK + MLA_KV_RANK:]
    nq = (cq * lax.rsqrt(jnp.mean(cq * cq, axis=-1, keepdims=True) + RMS_EPS) * gq_ref[...]).astype(BF16)
    nkv = (ckv * lax.rsqrt(jnp.mean(ckv * ckv, axis=-1, keepdims=True) + RMS_EPS) * gkv_ref[...]).astype(BF16)
    cos = cos_ref[...]
    sin = sin_ref[...]
    q1 = _dot(nq, wq_ref[...])
    q2 = _dot(nq, wqs_ref[...])
    k1 = _dot(nkv, wk_ref[0:MLA_KV_RANK, :]) + _dot(kr, wk_ref[MLA_KV_RANK:, :])
    k2 = _dot(kr, wks_ref[...])
    v = _dot(nkv, wv_ref[...]) + ones_ref[...]
    for h in range(MLA_HEADS):
        sl = slice(h * LANES, (h + 1) * LANES)
        q_ref[0, h] = (q1[:, sl] * cos + q2[:, sl] * sin).astype(BF16)
        k_ref[0, h] = (k1[:, sl] * cos + k2[:, sl] * sin).astype(BF16)
        vt_ref[0, h] = v[:, sl].T.astype(BF16)


def _mlap_call(mc, cos_t, sin_t, gq, gkv, wq, wqs, wk, wks, wv, ones, layer):
    b, ltot, _ = mc.shape
    tm = TOKEN_TILE
    lay = lambda a: _layer_spec(a, layer, 2)
    hm = jax.ShapeDtypeStruct((b, MLA_HEADS, ltot, LANES), BF16)
    hspec = pl.BlockSpec((1, MLA_HEADS, tm, LANES), lambda bi, i: (bi, 0, i, 0))
    return pl.pallas_call(
        _mlap_kernel,
        out_shape=(hm, hm, jax.ShapeDtypeStruct((b, MLA_HEADS, LANES, ltot), BF16)),
        grid=(b, ltot // tm),
        in_specs=[pl.BlockSpec((1, tm, mc.shape[2]), lambda bi, i: (bi, i, 0)),
                  pl.BlockSpec((tm, LANES), lambda bi, i: (i, 0)),
                  pl.BlockSpec((tm, LANES), lambda bi, i: (i, 0)),
                  lay(gq), lay(gkv), lay(wq), lay(wqs), lay(wk), lay(wks), lay(wv),
                  pl.BlockSpec(ones.shape, lambda bi, i: (0, 0))],
        out_specs=(hspec, hspec, pl.BlockSpec((1, MLA_HEADS, LANES, tm), lambda bi, i: (bi, 0, 0, i))),
        compiler_params=_params("parallel", "parallel"),
        name="mla_proj",
    )(mc, cos_t, sin_t, gq, gkv, wq, wqs, wk, wks, wv, ones)


def _fa_kernel(q_ref, k_ref, vt_ref, o_ref, *, tq, tk, nk):
    qs = [q_ref[0, hh] for hh in range(2)]

    def body(c, carry):
        off = pl.multiple_of(c * tk, tk)
        new = []
        for hh in range(2):
            m, acc = carry[2 * hh], carry[2 * hh + 1]
            kc = k_ref[0, hh, pl.ds(off, tk), :]
            vc = vt_ref[0, hh, :, pl.ds(off, tk)]
            s = _dot_nt(kc, qs[hh])
            mn = jnp.maximum(m, jnp.max(s, axis=0, keepdims=True))
            p = jnp.exp2(s - mn).astype(BF16)
            new += [mn, acc * jnp.exp2(m - mn) + _dot(vc, p)]
        return tuple(new)

    m0 = jnp.full((1, tq), NEG_INF, F32)
    acc0 = jnp.zeros((LANES, tq), F32)
    res = lax.fori_loop(0, nk, body, (m0, acc0, m0, acc0))
    o0 = res[1][0:HALF] / res[1][HALF:HALF + 1]
    o1 = res[3][HALF:LANES] / res[3][0:1]
    o_ref[0] = jnp.concatenate([o0, o1], axis=0).T.astype(BF16)


def _fa_call(q, k, vt, nq_rows, q_row0, lk, k_row0, tq, tk):
    b, h, _, _ = q.shape
    npair = h // 2
    kern = functools.partial(_fa_kernel, tq=tq, tk=tk, nk=lk // tk)
    qb0 = q_row0 // tq
    kb0 = k_row0 // lk
    return pl.pallas_call(
        kern,
        out_shape=jax.ShapeDtypeStruct((b, nq_rows, h * HALF), BF16),
        grid=(b, npair, nq_rows // tq),
        in_specs=[pl.BlockSpec((1, 2, tq, LANES), lambda bi, p, i: (bi, p, qb0 + i, 0)),
                  pl.BlockSpec((1, 2, lk, LANES), lambda bi, p, i: (bi, p, kb0, 0)),
                  pl.BlockSpec((1, 2, LANES, lk), lambda bi, p, i: (bi, p, 0, kb0))],
        out_specs=pl.BlockSpec((1, tq, LANES), lambda bi, p, i: (bi, i, p)),
        compiler_params=_params("parallel", "parallel", "parallel"),
        name="mla_attn",
    )(q, k, vt)


def _gelu_tanh(x):
    return 0.5 * x * (1.0 + jnp.tanh(math.sqrt(2.0 / math.pi) * (x + 0.044715 * (x * x * x))))


def _merge_kernel(y_ref, u_ref, z_ref, g_ref, ynl_ref, ync_ref, yml_ref, ymc_ref, xl_ref, xc_ref, mod_ref,
                  wglu_ref, bglu_ref, d_ref, wa_ref, wb_ref, wc_ref, wo_ref, lng_ref, lnb_ref, o_ref,
                  *, alpha, d_model, width, n_lat_tiles):
    ya = _gelu_tanh(y_ref[0] + d_ref[...] * u_ref[0])
    ya = ya * _sigmoid(_dot(ya.astype(BF16), wglu_ref[...]) + bglu_ref[...])
    z = z_ref[0]
    g = g_ref[0]
    yn = _pick_source(ynl_ref, ync_ref, n_lat_tiles)
    ym = _pick_source(yml_ref, ymc_ref, n_lat_tiles)
    ba = _dot((ya * z[:, 0:width].astype(F32)).astype(BF16), wa_ref[...])
    bn = _dot(yn * z[:, width:2 * width], wb_ref[...])
    bm = _dot(ym * z[:, 2 * width:3 * width], wc_ref[...])
    m = (g[:, 0:d_model].astype(F32) * ba + g[:, d_model:2 * d_model].astype(F32) * bn
         + g[:, 2 * d_model:3 * d_model].astype(F32) * bm)
    out = _dot(m.astype(BF16), wo_ref[...])
    t = alpha * _pick_source(xl_ref, xc_ref, n_lat_tiles) + mod_ref[0, 0][2:3] * out
    mu = jnp.mean(t, axis=-1, keepdims=True)
    tc = t - mu
    var = jnp.mean(tc * tc, axis=-1, keepdims=True)
    o_ref[0] = tc * lax.rsqrt(var + LN_EPS) * lng_ref[...] + lnb_ref[...]


def _merge_call(y, u, z, g, yn_lat, yn_ctx, ym_lat, ym_ctx, x_lat, x_ctx, x_ctx_tile0, modt,
                wglu, bglu, dskip, wa, wb, wc, wo, lng, lnb, n_lat_tiles, alpha, layer, n_out_tiles):
    b, _, d = x_lat.shape
    width = u.shape[2]
    tm = TOKEN_TILE
    tok = lambda a: pl.BlockSpec((1, tm, a.shape[2]), lambda bi, i: (bi, i, 0))
    lay = lambda a: _layer_spec(a, layer, 2)
    kern = functools.partial(_merge_kernel, alpha=alpha, d_model=d, width=width, n_lat_tiles=n_lat_tiles)
    return pl.pallas_call(
        kern,
        out_shape=jax.ShapeDtypeStruct((b, n_out_tiles * tm, d), F32),
        grid=(b, n_out_tiles),
        in_specs=[tok(y), tok(u), tok(z), tok(g)]
        + _two_source_specs(width, n_lat_tiles, 0) + _two_source_specs(width, n_lat_tiles, 0)
        + _two_source_specs(d, n_lat_tiles, x_ctx_tile0)
        + [pl.BlockSpec((None, 1, 1, 3, d), lambda bi, i: (layer, bi, i // n_lat_tiles, 0, 0)),
           lay(wglu), lay(bglu), lay(dskip), lay(wa), lay(wb), lay(wc), lay(wo), lay(lng), lay(lnb)],
        out_specs=pl.BlockSpec((1, tm, d), lambda bi, i: (bi, i, 0)),
        compiler_params=_params("parallel", "parallel"),
        name="merge_deepnorm",
    )(y, u, z, g, yn_lat, yn_ctx, ym_lat, ym_ctx, x_lat, x_ctx, modt,
      wglu, bglu, dskip, wa, wb, wc, wo, lng, lnb)


IN_PAD_AT = 6 * 512 + MLA_Q_RANK + MLA_KV_RANK + MLA_ROPE


def _wprep_kernel(w_ref, o_ref):
    w = w_ref[0]
    pad_end = IN_PAD_AT + LANES - MLA_ROPE
    o_ref[0, :, 0:1024] = w[:, 0:1024].astype(BF16)
    o_ref[0, :, 1024:1536] = (w[:, 1024:1536] * (NA_HEAD_DIM ** -0.5 * LOG2E)).astype(BF16)
    o_ref[0, :, 1536:IN_PAD_AT] = w[:, 1536:IN_PAD_AT].astype(BF16)
    o_ref[0, :, IN_PAD_AT:pad_end] = jnp.zeros((w.shape[0], pad_end - IN_PAD_AT), BF16)
    o_ref[0, :, pad_end:IN_WIDTH_PADDED] = w[:, IN_PAD_AT:].astype(BF16)


def _prep_w_in(w_in):
    depth, d, n = w_in.shape
    tr = TOKEN_TILE
    return pl.pallas_call(
        _wprep_kernel,
        out_shape=jax.ShapeDtypeStruct((depth, d, IN_WIDTH_PADDED), BF16),
        grid=(depth, d // tr),
        in_specs=[pl.BlockSpec((1, tr, n), lambda l, i: (l, i, 0))],
        out_specs=pl.BlockSpec((1, tr, IN_WIDTH_PADDED), lambda l, i: (l, i, 0)),
        compiler_params=_params("parallel", "parallel"),
        name="w_in_prep",
    )(w_in)


_ROPE_SWAP = np.concatenate([np.arange(8, 16), np.arange(0, 8), np.arange(24, 32), np.arange(16, 24)])


def _prep_mla(w_uq, w_ukv):
    h = MLA_HEADS
    scale = MLA_QK ** -0.5 * LOG2E
    wq = w_uq.reshape(MLA_Q_RANK, h, MLA_QK) * scale
    zq = jnp.zeros((MLA_Q_RANK, h, LANES - MLA_QK), F32)
    nope, pe = wq[:, :, :MLA_NOPE], wq[:, :, MLA_NOPE:]
    wq_main = jnp.concatenate([nope, pe, zq], axis=2).reshape(MLA_Q_RANK, h * LANES)
    wq_swap = jnp.concatenate([jnp.zeros_like(nope), pe[:, :, _ROPE_SWAP], zq], axis=2).reshape(MLA_Q_RANK, h * LANES)

    wkv = w_ukv.reshape(MLA_KV_RANK, h, MLA_NOPE + MLA_V)
    knope, wv = wkv[:, :, :MLA_NOPE], wkv[:, :, MLA_NOPE:]
    zk = jnp.zeros((MLA_KV_RANK, h, LANES - MLA_NOPE), F32)
    wk_top = jnp.concatenate([knope, zk], axis=2).reshape(MLA_KV_RANK, h * LANES)
    eye = np.zeros((LANES, LANES), np.float32)
    eye[np.arange(MLA_ROPE), MLA_NOPE + np.arange(MLA_ROPE)] = 1.0
    eye_sw = np.zeros((LANES, LANES), np.float32)
    eye_sw[_ROPE_SWAP, MLA_NOPE + np.arange(MLA_ROPE)] = 1.0
    wk_bot = jnp.asarray(np.tile(eye, (1, h)))
    wk_swap = jnp.asarray(np.tile(eye_sw, (1, h)))
    wk_main = jnp.concatenate([wk_top, wk_bot], axis=0)

    zv = jnp.zeros_like(wv)
    even = jnp.concatenate([wv, zv], axis=2)
    odd = jnp.concatenate([zv, wv], axis=2)
    is_even = (np.arange(h) % 2 == 0)[None, :, None]
    wv_full = jnp.where(is_even, even, odd).reshape(MLA_KV_RANK, h * LANES)
    return (wq_main.astype(BF16), wq_swap.astype(BF16), wk_main.astype(BF16), wk_swap.astype(BF16),
            wv_full.astype(BF16))


def _mla_ones_row():
    ones = np.zeros((1, MLA_HEADS, LANES), np.float32)
    ones[0, 0::2, MLA_V] = 1.0
    ones[0, 1::2, 0] = 1.0
    return jnp.asarray(ones.reshape(1, MLA_HEADS * LANES))


def _rope_tables(nlat, nctx):
    t = jnp.arange(nlat, dtype=jnp.int32)
    row = (t // GRID_W).astype(F32)
    col = (t % GRID_W).astype(F32)
    half = MLA_ROPE // 2
    inv = 1.0 / (ROPE_THETA ** (jnp.arange(0, half, 2, dtype=F32) / half))
    ar = row[:, None] * inv[None, :]
    ac = col[:, None] * inv[None, :]
    one = jnp.ones((nlat, MLA_NOPE), F32)
    tail = jnp.ones((nlat, LANES - MLA_QK), F32)
    cos_t = jnp.concatenate([one, jnp.cos(ar), jnp.cos(ar), jnp.cos(ac), jnp.cos(ac), tail], axis=1)
    sin_t = jnp.concatenate([0 * one, -jnp.sin(ar), jnp.sin(ar), -jnp.sin(ac), jnp.sin(ac), 0 * tail], axis=1)
    cos_t = jnp.concatenate([cos_t, jnp.ones((nctx, LANES), F32)], axis=0)
    sin_t = jnp.concatenate([sin_t, jnp.zeros((nctx, LANES), F32)], axis=0)
    return cos_t, sin_t


def _na_bias_tables(rpb):
    depth, h = rpb.shape[:2]
    wc = NA_WIN_COLS
    col = np.arange(GRID_W)
    col_start = np.clip(col - wc // 2, 0, GRID_W - wc)
    in_win = (col[None, :] >= col_start[:, None]) & (col[None, :] < col_start[:, None] + wc)
    idx_c = np.clip(col[None, :] - col[:, None], -(wc - 1), wc - 1) + wc - 1
    oh_c = (idx_c.T[:, :, None] == np.arange(2 * wc - 1)).astype(np.float32)
    bt = jnp.einsum('lhdc,kqc->lhdkq', rpb, oh_c, precision=lax.Precision.HIGHEST)
    bt = jnp.where(in_win.T, bt * LOG2E, NEG_INF)
    pt = jnp.concatenate([bt, jnp.roll(bt, 1, axis=2)], axis=-1)
    return pt.reshape(depth * h, 2 * NA_WIN_ROWS - 1, GRID_W, 2 * GRID_W).astype(F32)


def _s5_mats(lam_re, lam_im, log_dt, b_re, b_im, c_re, c_im):
    t = SSM_CHUNK
    g, p = lam_re.shape[1], lam_re.shape[2]
    gi = b_re.shape[-1]
    gb = LANES // gi
    nblk = g // gb
    dt = jnp.exp(log_dt.astype(F32))[..., None]
    lr, li = lam_re.astype(F32), lam_im.astype(F32)
    mag = jnp.exp(lr * dt)
    ab_re, ab_im = mag * jnp.cos(li * dt), mag * jnp.sin(li * dt)
    nr = ab_re - 1.0
    den = lr * lr + li * li
    fr = (nr * lr + ab_im * li) / den
    fi = (ab_im * lr - nr * li) / den
    bb_re = fr[..., None] * b_re - fi[..., None] * b_im
    bb_im = fr[..., None] * b_im + fi[..., None] * b_re
    j = jnp.arange(t + 1, dtype=F32)[None, None, :, None]
    pmag = jnp.exp(lr[:, :, None, :] * dt[:, :, None, :] * j)
    ang = li[:, :, None, :] * dt[:, :, None, :] * j
    ap_re, ap_im = pmag * jnp.cos(ang), pmag * jnp.sin(ang)

    ca_re = c_re[:, :, None] * ap_re[:, :, :, None, :] - c_im[:, :, None] * ap_im[:, :, :, None, :]
    ca_im = c_re[:, :, None] * ap_im[:, :, :, None, :] + c_im[:, :, None] * ap_re[:, :, :, None, :]
    bt_re = jnp.swapaxes(bb_re, 2, 3)[:, :, None, None]
    bt_im = jnp.swapaxes(bb_im, 2, 3)[:, :, None, None]
    kmat = jnp.sum(ca_re[:, :, :, :, None, :] * bt_re - ca_im[:, :, :, :, None, :] * bt_im, axis=-1)
    seq = jnp.concatenate([jnp.flip(kmat[1][:, 1:t], axis=1), (kmat[0][:, 0:1] + kmat[1][:, 0:1]), kmat[0][:, 1:t]],
                          axis=1)
    toep = jnp.stack([seq[:, t - 1 - k:2 * t - 1 - k] for k in range(t)], axis=1)
    toep = jnp.transpose(toep, (0, 1, 4, 2, 3))

    def powers(v, d, lo, reverse):
        v = v[d][:, lo:lo + t]
        return jnp.flip(v, axis=1) if reverse else v

    def end_map(d, reverse):
        are, aim = powers(ap_re, d, 0, reverse), powers(ap_im, d, 0, reverse)
        bre, bim = jnp.transpose(bb_re[d], (0, 2, 1)), jnp.transpose(bb_im[d], (0, 2, 1))
        mre = are[:, :, None, :] * bre[:, None] - aim[:, :, None, :] * bim[:, None]
        mim = are[:, :, None, :] * bim[:, None] + aim[:, :, None, :] * bre[:, None]
        return mre, mim

    ef_re, ef_im = end_map(0, True)
    eb_re, eb_im = end_map(1, False)
    mend = jnp.stack([ef_re, ef_im, eb_re, eb_im], axis=3)

    def out_map(d, reverse):
        cre = jnp.transpose(powers(ca_re, d, 1, reverse), (0, 3, 1, 2))
        cim = jnp.transpose(powers(ca_im, d, 1, reverse), (0, 3, 1, 2))
        return cre, -cim

    of_re, of_im = out_map(0, False)
    ob_re, ob_im = out_map(1, True)
    mout = jnp.stack([of_re, of_im, ob_re, ob_im], axis=1)

    def compact(m, rows):
        m = m.reshape(nblk, gb, m.shape[1], m.shape[2], -1)
        return jnp.transpose(m, (0, 2, 1, 3, 4)).reshape(nblk, rows, m.shape[-1]).astype(BF16)

    wc = compact(mend, t * gb * gi)
    tc = compact(toep, t * gb * gi)
    mc = compact(mout, 4 * gb * p)

    a16 = jnp.stack([ap_re[0][:, t], ap_im[0][:, t], ap_re[1][:, t], ap_im[1][:, t]], axis=1)
    a_blk = jnp.transpose(a16.reshape(nblk, gb, 4, p), (0, 2, 1, 3)).reshape(nblk, 4, gb * p)
    return wc, tc, mc, a_blk


def _s5_replication():
    def rep(nx, ny):
        r = np.zeros((nx, ny, nx, S5_GB, ny), np.float32)
        ix, iy = np.meshgrid(np.arange(nx), np.arange(ny), indexing='ij')
        r[ix, iy, ix, :, iy] = 1.0
        return jnp.asarray(r.reshape(nx * ny, nx * S5_GB * ny), dtype=BF16)
    return rep(4, SSM_STATE), rep(SSM_CHUNK, SSM_GROUP)


def kernel(x, c, ctx, c_ctx, w_mod, b_mod, w_in, ssm_lam_re, ssm_lam_im, ssm_log_dt, ssm_b_re, ssm_b_im,
           ssm_c_re, ssm_c_im, ssm_d, ssm_w_glu, ssm_b_glu, na_rpb, mla_q_norm, mla_w_uq, mla_kv_norm,
           mla_w_ukv, w_branch_a, w_branch_b, w_branch_c, w_out, ln_g, ln_b):
    b, nlat, d = x.shape
    nctx = ctx.shape[1]
    ltot = nlat + nctx
    depth = w_mod.shape[0]
    assert nlat % TOKEN_TILE == 0 and nctx % TOKEN_TILE == 0 and nlat % nctx == 0
    assert nlat % (NA_ROW_BLOCK * GRID_W) == 0 and b + 1 <= 8
    assert nlat % (S5_SUB * SSM_CHUNK) == 0 and nctx % (S5_SUB * SSM_CHUNK) == 0
    n_lat_tiles = nlat // TOKEN_TILE
    alpha = (2 * depth) ** 0.25
    fa_tq = FA_TQ if nlat % FA_TQ == 0 else nctx
    fa_tk = FA_TK if ltot % FA_TK == 0 else nctx

    cc = jnp.concatenate([c, c_ctx[None], jnp.zeros((8 - b - 1, d), F32)], axis=0)
    mod_all = _mod_call(cc, w_mod, b_mod)
    mod_lat = mod_all[:, :b].reshape(depth, b, 1, 3, d)
    mod_ctx = jnp.broadcast_to(mod_all[:, b].reshape(depth, 1, 1, 3, d), (depth, b, 1, 3, d))
    modt = jnp.concatenate([mod_lat, mod_ctx], axis=2)

    cos_t, sin_t = _rope_tables(nlat, nctx)

    w_pad = _prep_w_in(w_in)
    s5w = jax.vmap(_s5_mats)(ssm_lam_re, ssm_lam_im, ssm_log_dt, ssm_b_re, ssm_b_im, ssm_c_re, ssm_c_im)
    s5_wc, s5_tc, s5_mc, s5_a = (v.reshape((-1,) + v.shape[2:]) for v in s5w)
    rep_state, rep_out = _s5_replication()
    bias_all = _na_bias_tables(na_rpb)
    wq, wqs, wk, wks, wv = jax.vmap(_prep_mla)(mla_w_uq, mla_w_ukv)
    ones = _mla_ones_row()
    bf = lambda w: w.astype(BF16)
    row = lambda v: v.reshape(depth, 1, v.shape[-1])
    wglu, wa, wb, wc, wo = bf(ssm_w_glu), bf(w_branch_a), bf(w_branch_b), bf(w_branch_c), bf(w_out)
    bglu, dskip, lng, lnb = row(ssm_b_glu), row(ssm_d), row(ln_g), row(ln_b)
    gq, gkv = row(mla_q_norm), row(mla_kv_norm)

    x_lat, x_ctx, x_ctx_tile0 = x, ctx, 0
    for l in range(depth):
        n_out_tiles = n_lat_tiles if l == depth - 1 else ltot // TOKEN_TILE
        u, z, na, mc, g = _in_call(x_lat, x_ctx, x_ctx_tile0, ltot, modt, w_pad, n_lat_tiles, l)
        y = _s5_call(u, s5_wc, s5_tc, s5_mc, s5_a, rep_state, rep_out, nlat, nctx, l)
        yn_lat, yn_ctx = _na_call(na, bias_all, nlat, nctx, l), _na_ctx_call(na, nlat, nctx)
        q, k, vt = _mlap_call(mc, cos_t, sin_t, gq, gkv, wq, wqs, wk, wks, wv, ones, l)
        ym_lat = _fa_call(q, k, vt, nlat, 0, ltot, 0, fa_tq, fa_tk)
        ym_ctx = _fa_call(q, k, vt, nctx, nlat, nctx, nlat, nctx, nctx)
        xcat = _merge_call(y, u, z, g, yn_lat, yn_ctx, ym_lat, ym_ctx, x_lat, x_ctx, x_ctx_tile0, modt,
                           wglu, bglu, dskip, wa, wb, wc, wo, lng, lnb, n_lat_tiles, alpha, l, n_out_tiles)
        x_lat, x_ctx, x_ctx_tile0 = xcat, xcat, n_lat_tiles
    return xcat
```

```python
import functools
import math

import jax
import jax.numpy as jnp
import numpy as np
from jax import lax
from jax.experimental import pallas as pl
from jax.experimental.pallas import tpu as pltpu

F32 = jnp.float32
BF16 = jnp.bfloat16

GRID_W = 64
SSM_GROUP = 16
SSM_STATE = 64
SSM_CHUNK = 16
NA_HEADS = 8
NA_HEAD_DIM = 64
NA_WIN_ROWS = 8
NA_WIN_COLS = 16
MLA_HEADS = 8
MLA_Q_RANK = 256
MLA_KV_RANK = 128
MLA_NOPE = 64
MLA_ROPE = 32
MLA_V = 64
MLA_QK = MLA_NOPE + MLA_ROPE
ROPE_THETA = 10000.0
LN_EPS = 1e-5
RMS_EPS = 1e-6
NEG_INF = -1e30
LOG2E = math.log2(math.e)
LANES = 128
HALF = 64

TOKEN_TILE = 256
NA_ROW_BLOCK = 8
FA_TQ = 1024
FA_TK = 1408
VMEM_LIMIT = 56 * 1024 * 1024


def _sigmoid(x):
    return 1.0 / (1.0 + jnp.exp(-x))


def _dot(a, b):
    return jnp.dot(a, b, preferred_element_type=F32)


def _dot_nt(a, b):
    return lax.dot_general(a, b, (((1,), (1,)), ((), ())), preferred_element_type=F32)


def _params(*sem):
    return pltpu.CompilerParams(dimension_semantics=sem, vmem_limit_bytes=VMEM_LIMIT)


def _layer_spec(arr, layer, grid_rank):
    zeros = (0,) * (arr.ndim - 1)
    if grid_rank == 2:
        imap = lambda a, b: (layer,) + zeros
    else:
        imap = lambda a, b, c: (layer,) + zeros
    return pl.BlockSpec((None,) + arr.shape[1:], imap)


def _mod_kernel(cc_ref, w_ref, b_ref, o_ref):
    cc = cc_ref[...]
    s = (cc * _sigmoid(cc)).astype(BF16)
    o_ref[0] = _dot(s, w_ref[0].astype(BF16)) + b_ref[0]


def _mod_call(cc, w_mod, b_mod):
    depth, d, n = w_mod.shape
    tn = 1024
    return pl.pallas_call(
        _mod_kernel,
        out_shape=jax.ShapeDtypeStruct((depth, 8, n), F32),
        grid=(depth, n // tn),
        in_specs=[pl.BlockSpec((8, d), lambda l, j: (0, 0)),
                  pl.BlockSpec((1, d, tn), lambda l, j: (l, 0, j)),
                  pl.BlockSpec((1, 1, tn), lambda l, j: (l, 0, j))],
        out_specs=pl.BlockSpec((1, 8, tn), lambda l, j: (l, 0, j)),
        compiler_params=_params("parallel", "parallel"),
        name="adaln_mod",
    )(cc, w_mod, b_mod.reshape(depth, 1, n))


IN_WIDTH_PADDED = 7168
OUT_COLS = (512, 1536, 1536, 512, 3072)


def _two_source_specs(width, n_lat_tiles, ctx_tile0):
    tm = TOKEN_TILE
    lat = pl.BlockSpec((1, tm, width), lambda bi, i: (bi, jnp.minimum(i, n_lat_tiles - 1), 0))
    ctx = pl.BlockSpec((1, tm, width), lambda bi, i: (bi, ctx_tile0 + jnp.maximum(i - n_lat_tiles, 0), 0))
    return [lat, ctx]


def _pick_source(lat_ref, ctx_ref, n_lat_tiles):
    return jnp.where(pl.program_id(1) >= n_lat_tiles, ctx_ref[0], lat_ref[0])


def _in_kernel(xl_ref, xc_ref, mod_ref, w_ref, o_u, o_z, o_na, o_mc, o_g, *, n_lat_tiles):
    x = _pick_source(xl_ref, xc_ref, n_lat_tiles)
    mu = jnp.mean(x, axis=-1, keepdims=True)
    xc = x - mu
    var = jnp.mean(xc * xc, axis=-1, keepdims=True)
    mod = mod_ref[0, 0]
    h = (xc * lax.rsqrt(var + LN_EPS) * (1.0 + mod[1:2]) + mod[0:1]).astype(BF16)

    def mm(lo, hi):
        return _dot(h, w_ref[:, lo:hi])

    def silu(lo, hi):
        z = mm(lo, hi)
        return (z * _sigmoid(z)).astype(BF16)

    o_u[0] = mm(0, 512)
    o_z[0, :, 0:512] = silu(512, 1024)
    o_na[0] = mm(1024, 2560).astype(BF16)
    o_z[0, :, 512:1024] = silu(2560, 3072)
    o_mc[0] = mm(3072, 3584).astype(BF16)
    o_z[0, :, 1024:1536] = silu(3584, 4096)
    o_g[0] = _sigmoid(mm(4096, 7168)).astype(BF16)


def _in_call(x_lat, x_ctx, ctx_tile0, ltot, modt, w_pad, n_lat_tiles, layer):
    b, _, d = x_lat.shape
    tm = TOKEN_TILE
    tok = lambda w: pl.BlockSpec((1, tm, w), lambda bi, i: (bi, i, 0))
    dts = (F32, BF16, BF16, BF16, BF16)
    return pl.pallas_call(
        functools.partial(_in_kernel, n_lat_tiles=n_lat_tiles),
        out_shape=tuple(jax.ShapeDtypeStruct((b, ltot, w), dt) for w, dt in zip(OUT_COLS, dts)),
        grid=(b, ltot // tm),
        in_specs=_two_source_specs(d, n_lat_tiles, ctx_tile0) + [
            pl.BlockSpec((None, 1, 1, 3, d), lambda bi, i: (layer, bi, i // n_lat_tiles, 0, 0)),
            pl.BlockSpec((None,) + w_pad.shape[1:], lambda bi, i: (layer, 0, 0), pipeline_mode=pl.Buffered(1))],
        out_specs=tuple(tok(w) for w in OUT_COLS),
        compiler_params=_params("parallel", "parallel"),
        name="ln_in_proj",
    )(x_lat, x_ctx, modt, w_pad)


S5_SUB = 8
S5_GB = LANES // SSM_GROUP
S5_STATE_COLS = S5_GB * SSM_STATE
S5_EXPAND_CHUNK = 512


def _s5_gather_chunks(u_ref, nch):
    cols = [u_ref[0, pl.ds(t, nch, stride=SSM_CHUNK), :].astype(BF16) for t in range(SSM_CHUNK)]
    return jnp.concatenate(cols, axis=1)


def _s5_expand(c_ref, rep_ref, w_scr, row_unit, col_unit):
    comp = c_ref[0]
    nrow, ncol = w_scr.shape
    ch = S5_EXPAND_CHUNK
    rg = (lax.broadcasted_iota(jnp.int32, (nrow, ch), 0) // row_unit) % S5_GB
    for j in range(ncol // ch):
        cg = ((lax.broadcasted_iota(jnp.int32, (nrow, ch), 1) + j * ch) // col_unit) % S5_GB
        full = _dot(comp, rep_ref[:, j * ch:(j + 1) * ch])
        w_scr[:, j * ch:(j + 1) * ch] = jnp.where(rg == cg, full, 0.0).astype(BF16)


def _s5_state_kernel(u_ref, wc_ref, rep_ref, a_ref, s_ref, we_scr, e_scr, s_scr, *, nlat, nctx):
    ntot = nlat + nctx
    w = S5_STATE_COLS

    @pl.when(pl.program_id(1) == 0)
    def _():
        _s5_expand(wc_ref, rep_ref, we_scr, SSM_GROUP, SSM_STATE)

    e_scr[...] = _dot(_s5_gather_chunks(u_ref, ntot), we_scr[...])
    a = a_ref[0]
    arf, aif, arb, aib = a[0:1], a[1:2], a[2:3], a[3:4]
    ntile, nctx_t, nlat_t = ntot // S5_SUB, nctx // S5_SUB, nlat // S5_SUB

    def sweep(er, ei, sr, si, ar, ai, order):
        rows_r, rows_i = [None] * S5_SUB, [None] * S5_SUB
        for k in order:
            rows_r[k], rows_i[k] = sr, si
            sr, si = ar * sr - ai * si + er[k:k + 1], ar * si + ai * sr + ei[k:k + 1]
        return jnp.concatenate(rows_r, axis=0), jnp.concatenate(rows_i, axis=0), sr, si

    def body(i, carry):
        srf, sif, srb, sib = carry
        rf = pl.multiple_of(jnp.where(i < nctx_t, nlat_t + i, i - nctx_t) * S5_SUB, S5_SUB)
        rb = pl.multiple_of((ntile - 1 - i) * S5_SUB, S5_SUB)
        pfr, pfi, srf, sif = sweep(e_scr[pl.ds(rf, S5_SUB), 0:w], e_scr[pl.ds(rf, S5_SUB), w:2 * w],
                                   srf, sif, arf, aif, range(S5_SUB))
        pbr, pbi, srb, sib = sweep(e_scr[pl.ds(rb, S5_SUB), 2 * w:3 * w], e_scr[pl.ds(rb, S5_SUB), 3 * w:4 * w],
                                   srb, sib, arb, aib, range(S5_SUB - 1, -1, -1))
        s_scr[pl.ds(rf, S5_SUB), 0:w] = pfr
        s_scr[pl.ds(rf, S5_SUB), w:2 * w] = pfi
        s_scr[pl.ds(rb, S5_SUB), 2 * w:3 * w] = pbr
        s_scr[pl.ds(rb, S5_SUB), 3 * w:4 * w] = pbi
        return srf, sif, srb, sib

    zero = jnp.zeros((1, w), F32)
    lax.fori_loop(0, ntile, body, (zero, zero, zero, zero))
    s_ref[0, 0] = s_scr[...].astype(BF16)


def _s5_out_kernel(u_ref, s_ref, tc_ref, mc_ref, rep_ref, y_ref, wy_scr, mo_scr, *, nch):
    @pl.when(pl.program_id(1) == 0)
    def _():
        _s5_expand(tc_ref, rep_ref, wy_scr, SSM_GROUP, SSM_GROUP)
        _s5_expand(mc_ref, rep_ref, mo_scr, SSM_STATE, SSM_GROUP)

    lhs = _s5_gather_chunks(u_ref, nch)
    s = s_ref[0, 0]
    half = SSM_CHUNK // 2
    for hf in range(2):
        cols = slice(hf * half * LANES, (hf + 1) * half * LANES)
        y = _dot(lhs, wy_scr[:, cols]) + _dot(s, mo_scr[:, cols])
        for t in range(half):
            y_ref[0, pl.ds(hf * half + t, nch, stride=SSM_CHUNK), :] = y[:, t * LANES:(t + 1) * LANES]


def _s5_call(u, wc, tc, mc, a, rep_state, rep_out, nlat, nctx, layer):
    b, ltot, width = u.shape
    nblk = width // LANES
    nch = ltot // SSM_CHUNK
    kdim = SSM_CHUNK * LANES
    scols = 4 * S5_STATE_COLS
    ublk = pl.BlockSpec((1, ltot, LANES), lambda j, bi: (bi, 0, j))
    sblk = pl.BlockSpec((1, 1, nch, scols), lambda j, bi: (bi, j, 0, 0))
    cblk = lambda c: pl.BlockSpec((1,) + c.shape[1:], lambda j, bi: (layer * nblk + j, 0, 0))
    rblk = lambda r: pl.BlockSpec(r.shape, lambda j, bi: (0, 0))
    sem = ("parallel", "arbitrary")
    states = pl.pallas_call(
        functools.partial(_s5_state_kernel, nlat=nlat // SSM_CHUNK, nctx=nctx // SSM_CHUNK),
        out_shape=jax.ShapeDtypeStruct((b, nblk, nch, scols), BF16),
        grid=(nblk, b),
        in_specs=[ublk, cblk(wc), rblk(rep_state), cblk(a)],
        out_specs=sblk,
        scratch_shapes=[pltpu.VMEM((kdim, scols), BF16), pltpu.VMEM((nch, scols), F32),
                        pltpu.VMEM((nch, scols), F32)],
        compiler_params=_params(*sem),
        name="s5_states",
    )(u, wc, rep_state, a)
    return pl.pallas_call(
        functools.partial(_s5_out_kernel, nch=nch),
        out_shape=jax.ShapeDtypeStruct((b, ltot, width), F32),
        grid=(nblk, b),
        in_specs=[ublk, sblk, cblk(tc), cblk(mc), rblk(rep_out)],
        out_specs=ublk,
        scratch_shapes=[pltpu.VMEM((kdim, kdim), BF16), pltpu.VMEM((scols, kdim), BF16)],
        compiler_params=_params(*sem),
        name="s5_outputs",
    )(u, states, tc, mc, rep_out)


NA_UNION = NA_ROW_BLOCK + NA_WIN_ROWS - 1


def _na_kernel(q_ref, k_ref, v_ref, pt_ref, o_ref, bias_scr, *, nrows, nlat, nctx):
    i = pl.program_id(2)
    nblk = nrows // NA_ROW_BLOCK
    r0 = i * NA_ROW_BLOCK
    half_win = NA_WIN_ROWS // 2
    ustart = jnp.clip(r0 - half_win, 0, nrows - NA_UNION)
    nkl = NA_UNION * GRID_W

    @pl.when((i <= 1) | (i == nblk - 1))
    def _():
        lane = lax.broadcasted_iota(jnp.int32, (1, LANES), 1)
        for rp in range(NA_ROW_BLOCK // 2):
            ra = r0 + 2 * rp
            sa = jnp.clip(ra - half_win, 0, nrows - NA_WIN_ROWS)
            sb = jnp.clip(ra + 1 - half_win, 0, nrows - NA_WIN_ROWS)
            for u in range(NA_UNION):
                kr = ustart + u
                pen_a = jnp.where((kr >= sa) & (kr < sa + NA_WIN_ROWS), 0.0, NEG_INF)
                pen_b = jnp.where((kr >= sb) & (kr < sb + NA_WIN_ROWS), 0.0, NEG_INF)
                pen = jnp.where(lane < HALF, pen_a, pen_b)
                d = jnp.clip(kr - ra + NA_WIN_ROWS - 1, 1, 2 * NA_WIN_ROWS - 2)
                for hh in range(2):
                    bias_scr[hh, u * GRID_W:(u + 1) * GRID_W, rp * LANES:(rp + 1) * LANES] = pt_ref[hh, d] + pen

    off = pl.multiple_of(ustart * GRID_W, GRID_W)
    kw = k_ref[0, pl.ds(off, nkl), :]
    kc = k_ref[0, nlat:nlat + nctx, :]
    vwt = v_ref[0, pl.ds(off, nkl), :].T
    vct = v_ref[0, nlat:nlat + nctx, :].T
    q = q_ref[0]
    lane = lax.broadcasted_iota(jnp.int32, q.shape, 1)
    outs = []
    for hh in range(2):
        qm = jnp.where(lane < HALF if hh == 0 else lane >= HALF, q, jnp.zeros_like(q))
        sw = _dot_nt(kw, qm) + bias_scr[hh]
        sc = _dot_nt(kc, qm)
        m = jnp.maximum(jnp.max(sw, axis=0, keepdims=True), jnp.max(sc, axis=0, keepdims=True))
        pw = jnp.exp2(sw - m)
        pc = jnp.exp2(sc - m)
        den = jnp.sum(pw, axis=0, keepdims=True) + jnp.sum(pc, axis=0, keepdims=True)
        o = _dot(vwt, pw.astype(BF16)) + _dot(vct, pc.astype(BF16))
        outs.append(o / den)
    ot = jnp.concatenate([outs[0][0:HALF], outs[1][HALF:LANES]], axis=0)
    o_ref[0] = ot.T.astype(BF16)


def _na_call(na, pt_all, nlat, nctx, layer):
    b, ltot, _ = na.shape
    nrows = nlat // GRID_W
    rb = NA_ROW_BLOCK
    npair = NA_HEADS // 2
    kern = functools.partial(_na_kernel, nrows=nrows, nlat=nlat, nctx=nctx)
    return pl.pallas_call(
        kern,
        out_shape=jax.ShapeDtypeStruct((b, nlat, NA_HEADS * NA_HEAD_DIM), BF16),
        grid=(b, npair, nrows // rb),
        in_specs=[pl.BlockSpec((1, rb * GRID_W, LANES), lambda bi, p, i: (bi, i, p)),
                  pl.BlockSpec((1, ltot, LANES), lambda bi, p, i: (bi, 0, npair + p)),
                  pl.BlockSpec((1, ltot, LANES), lambda bi, p, i: (bi, 0, 2 * npair + p)),
                  pl.BlockSpec((2, 2 * NA_WIN_ROWS - 1, GRID_W, LANES),
                               lambda bi, p, i: (layer * npair + p, 0, 0, 0))],
        out_specs=pl.BlockSpec((1, rb * GRID_W, LANES), lambda bi, p, i: (bi, i, p)),
        scratch_shapes=[pltpu.VMEM((2, NA_UNION * GRID_W, rb * GRID_W), F32)],
        compiler_params=_params("parallel", "parallel", "arbitrary"),
        name="na_latent",
    )(na, na, na, pt_all)


def _na_ctx_kernel(q_ref, k_ref, v_ref, o_ref):
    n = q_ref.shape[1]
    lane = lax.broadcasted_iota(jnp.int32, (n, LANES), 1)
    lo = lane < HALF
    hi = lane >= HALF
    q = q_ref[0]
    k = k_ref[0]
    v = v_ref[0]
    outs = []
    for hh in range(2):
        qm = jnp.where(lo if hh == 0 else hi, q, jnp.zeros_like(q))
        s = _dot_nt(qm, k)
        p = jnp.exp2(s - jnp.max(s, axis=-1, keepdims=True))
        outs.append(_dot(p.astype(BF16), v) / jnp.sum(p, axis=-1, keepdims=True))
    o_ref[0] = jnp.where(lo, outs[0], outs[1]).astype(BF16)


def _na_ctx_call(na, nlat, nctx):
    b = na.shape[0]
    npair = NA_HEADS // 2
    rblk = nlat // nctx
    return pl.pallas_call(
        _na_ctx_kernel,
        out_shape=jax.ShapeDtypeStruct((b, nctx, NA_HEADS * NA_HEAD_DIM), BF16),
        grid=(b, npair),
        in_specs=[pl.BlockSpec((1, nctx, LANES), lambda bi, p: (bi, rblk, p)),
                  pl.BlockSpec((1, nctx, LANES), lambda bi, p: (bi, rblk, npair + p)),
                  pl.BlockSpec((1, nctx, LANES), lambda bi, p: (bi, rblk, 2 * npair + p))],
        out_specs=pl.BlockSpec((1, nctx, LANES), lambda bi, p: (bi, 0, p)),
        compiler_params=_params("parallel", "parallel"),
        name="na_context",
    )(na, na, na)


def _mlap_kernel(mc_ref, cos_ref, sin_ref, gq_ref, gkv_ref, wq_ref, wqs_ref, wk_ref, wks_ref,
                 wv_ref, ones_ref, q_ref, k_ref, vt_ref):
    mc = mc_ref[0]
    cq = mc[:, 0:MLA_Q_RANK].astype(F32)
    ckv = mc[:, MLA_Q_RANK:MLA_Q_RANK + MLA_KV_RANK].astype(F32)
    kr = mc[:, MLA_Q_RANK + MLA_KV_RANK:]
    nq = (cq * lax.rsqrt(jnp.mean(cq * cq, axis=-1, keepdims=True) + RMS_EPS) * gq_ref[...]).astype(BF16)
    nkv = (ckv * lax.rsqrt(jnp.mean(ckv * ckv, axis=-1, keepdims=True) + RMS_EPS) * gkv_ref[...]).astype(BF16)
    cos = cos_ref[...]
    sin = sin_ref[...]
    q1 = _dot(nq, wq_ref[...])
    q2 = _dot(nq, wqs_ref[...])
    k1 = _dot(nkv, wk_ref[0:MLA_KV_RANK, :]) + _dot(kr, wk_ref[MLA_KV_RANK:, :])
    k2 = _dot(kr, wks_ref[...])
    v = _dot(nkv, wv_ref[...]) + ones_ref[...]
    for h in range(MLA_HEADS):
        sl = slice(h * LANES, (h + 1) * LANES)
        q_ref[0, h] = (q1[:, sl] * cos + q2[:, sl] * sin).astype(BF16)
        k_ref[0, h] = (k1[:, sl] * cos + k2[:, sl] * sin).astype(BF16)
        vt_ref[0, h] = v[:, sl].T.astype(BF16)


def _mlap_call(mc, cos_t, sin_t, gq, gkv, wq, wqs, wk, wks, wv, ones, layer):
    b, ltot, _ = mc.shape
    tm = TOKEN_TILE
    lay = lambda a: _layer_spec(a, layer, 2)
    hm = jax.ShapeDtypeStruct((b, MLA_HEADS, ltot, LANES), BF16)
    hspec = pl.BlockSpec((1, MLA_HEADS, tm, LANES), lambda bi, i: (bi, 0, i, 0))
    return pl.pallas_call(
        _mlap_kernel,
        out_shape=(hm, hm, jax.ShapeDtypeStruct((b, MLA_HEADS, LANES, ltot), BF16)),
        grid=(b, ltot // tm),
        in_specs=[pl.BlockSpec((1, tm, mc.shape[2]), lambda bi, i: (bi, i, 0)),
                  pl.BlockSpec((tm, LANES), lambda bi, i: (i, 0)),
                  pl.BlockSpec((tm, LANES), lambda bi, i: (i, 0)),
                  lay(gq), lay(gkv), lay(wq), lay(wqs), lay(wk), lay(wks), lay(wv),
                  pl.BlockSpec(ones.shape, lambda bi, i: (0, 0))],
        out_specs=(hspec, hspec, pl.BlockSpec((1, MLA_HEADS, LANES, tm), lambda bi, i: (bi, 0, 0, i))),
        compiler_params=_params("parallel", "parallel"),
        name="mla_proj",
    )(mc, cos_t, sin_t, gq, gkv, wq, wqs, wk, wks, wv, ones)


def _fa_kernel(q_ref, k_ref, vt_ref, o_ref, sa_scr, sb_scr, *, tq, tk, nk):
    qs = [q_ref[0, hh] for hh in range(2)]

    def scores(c, s_scr):
        off = pl.multiple_of(c * tk, tk)
        for hh in range(2):
            s_scr[hh] = _dot_nt(k_ref[0, hh, pl.ds(off, tk), :], qs[hh])

    def softmax_pv(c, s_scr, carry):
        off = pl.multiple_of(c * tk, tk)
        new = []
        for hh in range(2):
            m, acc = carry[2 * hh], carry[2 * hh + 1]
            s = s_scr[hh]
            mn = jnp.maximum(m, jnp.max(s, axis=0, keepdims=True))
            p = jnp.exp2(s - mn).astype(BF16)
            new += [mn, acc * jnp.exp2(m - mn) + _dot(vt_ref[0, hh, :, pl.ds(off, tk)], p)]
        return tuple(new)

    def pair(i, carry):
        scores(2 * i + 1, sb_scr)
        carry = softmax_pv(2 * i, sa_scr, carry)
        scores(2 * i + 2, sa_scr)
        return softmax_pv(2 * i + 1, sb_scr, carry)

    m0 = jnp.full((1, tq), NEG_INF, F32)
    acc0 = jnp.zeros((LANES, tq), F32)
    res = (m0, acc0, m0, acc0)
    scores(0, sa_scr)
    npairs = (nk - 1) // 2
    res = lax.fori_loop(0, npairs, pair, res)
    c = 2 * npairs
    if nk - c == 2:
        scores(c + 1, sb_scr)
        res = softmax_pv(c, sa_scr, res)
        res = softmax_pv(c + 1, sb_scr, res)
    else:
        res = softmax_pv(c, sa_scr, res)
    o0 = res[1][0:HALF] / res[1][HALF:HALF + 1]
    o1 = res[3][HALF:LANES] / res[3][0:1]
    o_ref[0] = jnp.concatenate([o0, o1], axis=0).T.astype(BF16)


def _fa_call(q, k, vt, nq_rows, q_row0, lk, k_row0, tq, tk):
    b, h, _, _ = q.shape
    npair = h // 2
    kern = functools.partial(_fa_kernel, tq=tq, tk=tk, nk=lk // tk)
    qb0 = q_row0 // tq
    kb0 = k_row0 // lk
    return pl.pallas_call(
        kern,
        out_shape=jax.ShapeDtypeStruct((b, nq_rows, h * HALF), BF16),
        grid=(b, npair, nq_rows // tq),
        in_specs=[pl.BlockSpec((1, 2, tq, LANES), lambda bi, p, i: (bi, p, qb0 + i, 0)),
                  pl.BlockSpec((1, 2, lk, LANES), lambda bi, p, i: (bi, p, kb0, 0)),
                  pl.BlockSpec((1, 2, LANES, lk), lambda bi, p, i: (bi, p, 0, kb0))],
        out_specs=pl.BlockSpec((1, tq, LANES), lambda bi, p, i: (bi, i, p)),
        scratch_shapes=[pltpu.VMEM((2, tk, tq), F32), pltpu.VMEM((2, tk, tq), F32)],
        compiler_params=_params("parallel", "parallel", "parallel"),
        name="mla_attn",
    )(q, k, vt)


def _gelu_tanh(x):
    return 0.5 * x * (1.0 + jnp.tanh(math.sqrt(2.0 / math.pi) * (x + 0.044715 * (x * x * x))))


def _merge_kernel(y_ref, u_ref, z_ref, g_ref, ynl_ref, ync_ref, yml_ref, ymc_ref, xl_ref, xc_ref, mod_ref,
                  wglu_ref, bglu_ref, d_ref, wa_ref, wb_ref, wc_ref, wo_ref, lng_ref, lnb_ref, o_ref,
                  *, alpha, d_model, width, n_lat_tiles):
    ya = _gelu_tanh(y_ref[0] + d_ref[...] * u_ref[0])
    ya = ya * _sigmoid(_dot(ya.astype(BF16), wglu_ref[...]) + bglu_ref[...])
    z = z_ref[0]
    g = g_ref[0]
    yn = _pick_source(ynl_ref, ync_ref, n_lat_tiles)
    ym = _pick_source(yml_ref, ymc_ref, n_lat_tiles)
    ba = _dot((ya * z[:, 0:width].astype(F32)).astype(BF16), wa_ref[...])
    bn = _dot(yn * z[:, width:2 * width], wb_ref[...])
    bm = _dot(ym * z[:, 2 * width:3 * width], wc_ref[...])
    m = (g[:, 0:d_model].astype(F32) * ba + g[:, d_model:2 * d_model].astype(F32) * bn
         + g[:, 2 * d_model:3 * d_model].astype(F32) * bm)
    out = _dot(m.astype(BF16), wo_ref[...])
    t = alpha * _pick_source(xl_ref, xc_ref, n_lat_tiles) + mod_ref[0, 0][2:3] * out
    mu = jnp.mean(t, axis=-1, keepdims=True)
    tc = t - mu
    var = jnp.mean(tc * tc, axis=-1, keepdims=True)
    o_ref[0] = tc * lax.rsqrt(var + LN_EPS) * lng_ref[...] + lnb_ref[...]


def _merge_call(y, u, z, g, yn_lat, yn_ctx, ym_lat, ym_ctx, x_lat, x_ctx, x_ctx_tile0, modt,
                wglu, bglu, dskip, wa, wb, wc, wo, lng, lnb, n_lat_tiles, alpha, layer, n_out_tiles):
    b, _, d = x_lat.shape
    width = u.shape[2]
    tm = TOKEN_TILE
    tok = lambda a: pl.BlockSpec((1, tm, a.shape[2]), lambda bi, i: (bi, i, 0))
    lay = lambda a: _layer_spec(a, layer, 2)
    kern = functools.partial(_merge_kernel, alpha=alpha, d_model=d, width=width, n_lat_tiles=n_lat_tiles)
    return pl.pallas_call(
        kern,
        out_shape=jax.ShapeDtypeStruct((b, n_out_tiles * tm, d), F32),
        grid=(b, n_out_tiles),
        in_specs=[tok(y), tok(u), tok(z), tok(g)]
        + _two_source_specs(width, n_lat_tiles, 0) + _two_source_specs(width, n_lat_tiles, 0)
        + _two_source_specs(d, n_lat_tiles, x_ctx_tile0)
        + [pl.BlockSpec((None, 1, 1, 3, d), lambda bi, i: (layer, bi, i // n_lat_tiles, 0, 0)),
           lay(wglu), lay(bglu), lay(dskip), lay(wa), lay(wb), lay(wc), lay(wo), lay(lng), lay(lnb)],
        out_specs=pl.BlockSpec((1, tm, d), lambda bi, i: (bi, i, 0)),
        compiler_params=_params("parallel", "parallel"),
        name="merge_deepnorm",
    )(y, u, z, g, yn_lat, yn_ctx, ym_lat, ym_ctx, x_lat, x_ctx, modt,
      wglu, bglu, dskip, wa, wb, wc, wo, lng, lnb)


IN_PAD_AT = 6 * 512 + MLA_Q_RANK + MLA_KV_RANK + MLA_ROPE


def _wprep_kernel(w_ref, o_ref):
    w = w_ref[0]
    pad_end = IN_PAD_AT + LANES - MLA_ROPE
    o_ref[0, :, 0:1024] = w[:, 0:1024].astype(BF16)
    o_ref[0, :, 1024:1536] = (w[:, 1024:1536] * (NA_HEAD_DIM ** -0.5 * LOG2E)).astype(BF16)
    o_ref[0, :, 1536:IN_PAD_AT] = w[:, 1536:IN_PAD_AT].astype(BF16)
    o_ref[0, :, IN_PAD_AT:pad_end] = jnp.zeros((w.shape[0], pad_end - IN_PAD_AT), BF16)
    o_ref[0, :, pad_end:IN_WIDTH_PADDED] = w[:, IN_PAD_AT:].astype(BF16)


def _prep_w_in(w_in):
    depth, d, n = w_in.shape
    tr = TOKEN_TILE
    return pl.pallas_call(
        _wprep_kernel,
        out_shape=jax.ShapeDtypeStruct((depth, d, IN_WIDTH_PADDED), BF16),
        grid=(depth, d // tr),
        in_specs=[pl.BlockSpec((1, tr, n), lambda l, i: (l, i, 0))],
        out_specs=pl.BlockSpec((1, tr, IN_WIDTH_PADDED), lambda l, i: (l, i, 0)),
        compiler_params=_params("parallel", "parallel"),
        name="w_in_prep",
    )(w_in)


_ROPE_SWAP = np.concatenate([np.arange(8, 16), np.arange(0, 8), np.arange(24, 32), np.arange(16, 24)])


def _prep_mla(w_uq, w_ukv):
    h = MLA_HEADS
    scale = MLA_QK ** -0.5 * LOG2E
    wq = w_uq.reshape(MLA_Q_RANK, h, MLA_QK) * scale
    zq = jnp.zeros((MLA_Q_RANK, h, LANES - MLA_QK), F32)
    nope, pe = wq[:, :, :MLA_NOPE], wq[:, :, MLA_NOPE:]
    wq_main = jnp.concatenate([nope, pe, zq], axis=2).reshape(MLA_Q_RANK, h * LANES)
    wq_swap = jnp.concatenate([jnp.zeros_like(nope), pe[:, :, _ROPE_SWAP], zq], axis=2).reshape(MLA_Q_RANK, h * LANES)

    wkv = w_ukv.reshape(MLA_KV_RANK, h, MLA_NOPE + MLA_V)
    knope, wv = wkv[:, :, :MLA_NOPE], wkv[:, :, MLA_NOPE:]
    zk = jnp.zeros((MLA_KV_RANK, h, LANES - MLA_NOPE), F32)
    wk_top = jnp.concatenate([knope, zk], axis=2).reshape(MLA_KV_RANK, h * LANES)
    eye = np.zeros((LANES, LANES), np.float32)
    eye[np.arange(MLA_ROPE), MLA_NOPE + np.arange(MLA_ROPE)] = 1.0
    eye_sw = np.zeros((LANES, LANES), np.float32)
    eye_sw[_ROPE_SWAP, MLA_NOPE + np.arange(MLA_ROPE)] = 1.0
    wk_bot = jnp.asarray(np.tile(eye, (1, h)))
    wk_swap = jnp.asarray(np.tile(eye_sw, (1, h)))
    wk_main = jnp.concatenate([wk_top, wk_bot], axis=0)

    zv = jnp.zeros_like(wv)
    even = jnp.concatenate([wv, zv], axis=2)
    odd = jnp.concatenate([zv, wv], axis=2)
    is_even = (np.arange(h) % 2 == 0)[None, :, None]
    wv_full = jnp.where(is_even, even, odd).reshape(MLA_KV_RANK, h * LANES)
    return (wq_main.astype(BF16), wq_swap.astype(BF16), wk_main.astype(BF16), wk_swap.astype(BF16),
            wv_full.astype(BF16))


def _mla_ones_row():
    ones = np.zeros((1, MLA_HEADS, LANES), np.float32)
    ones[0, 0::2, MLA_V] = 1.0
    ones[0, 1::2, 0] = 1.0
    return jnp.asarray(ones.reshape(1, MLA_HEADS * LANES))


def _rope_tables(nlat, nctx):
    t = jnp.arange(nlat, dtype=jnp.int32)
    row = (t // GRID_W).astype(F32)
    col = (t % GRID_W).astype(F32)
    half = MLA_ROPE // 2
    inv = 1.0 / (ROPE_THETA ** (jnp.arange(0, half, 2, dtype=F32) / half))
    ar = row[:, None] * inv[None, :]
    ac = col[:, None] * inv[None, :]
    one = jnp.ones((nlat, MLA_NOPE), F32)
    tail = jnp.ones((nlat, LANES - MLA_QK), F32)
    cos_t = jnp.concatenate([one, jnp.cos(ar), jnp.cos(ar), jnp.cos(ac), jnp.cos(ac), tail], axis=1)
    sin_t = jnp.concatenate([0 * one, -jnp.sin(ar), jnp.sin(ar), -jnp.sin(ac), jnp.sin(ac), 0 * tail], axis=1)
    cos_t = jnp.concatenate([cos_t, jnp.ones((nctx, LANES), F32)], axis=0)
    sin_t = jnp.concatenate([sin_t, jnp.zeros((nctx, LANES), F32)], axis=0)
    return cos_t, sin_t


def _na_bias_tables(rpb):
    depth, h = rpb.shape[:2]
    wc = NA_WIN_COLS
    col = np.arange(GRID_W)
    col_start = np.clip(col - wc // 2, 0, GRID_W - wc)
    in_win = (col[None, :] >= col_start[:, None]) & (col[None, :] < col_start[:, None] + wc)
    idx_c = np.clip(col[None, :] - col[:, None], -(wc - 1), wc - 1) + wc - 1
    oh_c = (idx_c.T[:, :, None] == np.arange(2 * wc - 1)).astype(np.float32)
    bt = jnp.einsum('lhdc,kqc->lhdkq', rpb, oh_c, precision=lax.Precision.HIGHEST)
    bt = jnp.where(in_win.T, bt * LOG2E, NEG_INF)
    pt = jnp.concatenate([bt, jnp.roll(bt, 1, axis=2)], axis=-1)
    return pt.reshape(depth * h, 2 * NA_WIN_ROWS - 1, GRID_W, 2 * GRID_W).astype(F32)


def _s5_mats(lam_re, lam_im, log_dt, b_re, b_im, c_re, c_im):
    t = SSM_CHUNK
    g, p = lam_re.shape[1], lam_re.shape[2]
    gi = b_re.shape[-1]
    gb = LANES // gi
    nblk = g // gb
    dt = jnp.exp(log_dt.astype(F32))[..., None]
    lr, li = lam_re.astype(F32), lam_im.astype(F32)
    mag = jnp.exp(lr * dt)
    ab_re, ab_im = mag * jnp.cos(li * dt), mag * jnp.sin(li * dt)
    nr = ab_re - 1.0
    den = lr * lr + li * li
    fr = (nr * lr + ab_im * li) / den
    fi = (ab_im * lr - nr * li) / den
    bb_re = fr[..., None] * b_re - fi[..., None] * b_im
    bb_im = fr[..., None] * b_im + fi[..., None] * b_re
    j = jnp.arange(t + 1, dtype=F32)[None, None, :, None]
    pmag = jnp.exp(lr[:, :, None, :] * dt[:, :, None, :] * j)
    ang = li[:, :, None, :] * dt[:, :, None, :] * j
    ap_re, ap_im = pmag * jnp.cos(ang), pmag * jnp.sin(ang)

    ca_re = c_re[:, :, None] * ap_re[:, :, :, None, :] - c_im[:, :, None] * ap_im[:, :, :, None, :]
    ca_im = c_re[:, :, None] * ap_im[:, :, :, None, :] + c_im[:, :, None] * ap_re[:, :, :, None, :]
    bt_re = jnp.swapaxes(bb_re, 2, 3)[:, :, None, None]
    bt_im = jnp.swapaxes(bb_im, 2, 3)[:, :, None, None]
    kmat = jnp.sum(ca_re[:, :, :, :, None, :] * bt_re - ca_im[:, :, :, :, None, :] * bt_im, axis=-1)
    seq = jnp.concatenate([jnp.flip(kmat[1][:, 1:t], axis=1), (kmat[0][:, 0:1] + kmat[1][:, 0:1]), kmat[0][:, 1:t]],
                          axis=1)
    toep = jnp.stack([seq[:, t - 1 - k:2 * t - 1 - k] for k in range(t)], axis=1)
    toep = jnp.transpose(toep, (0, 1, 4, 2, 3))

    def powers(v, d, lo, reverse):
        v = v[d][:, lo:lo + t]
        return jnp.flip(v, axis=1) if reverse else v

    def end_map(d, reverse):
        are, aim = powers(ap_re, d, 0, reverse), powers(ap_im, d, 0, reverse)
        bre, bim = jnp.transpose(bb_re[d], (0, 2, 1)), jnp.transpose(bb_im[d], (0, 2, 1))
        mre = are[:, :, None, :] * bre[:, None] - aim[:, :, None, :] * bim[:, None]
        mim = are[:, :, None, :] * bim[:, None] + aim[:, :, None, :] * bre[:, None]
        return mre, mim

    ef_re, ef_im = end_map(0, True)
    eb_re, eb_im = end_map(1, False)
    mend = jnp.stack([ef_re, ef_im, eb_re, eb_im], axis=3)

    def out_map(d, reverse):
        cre = jnp.transpose(powers(ca_re, d, 1, reverse), (0, 3, 1, 2))
        cim = jnp.transpose(powers(ca_im, d, 1, reverse), (0, 3, 1, 2))
        return cre, -cim

    of_re, of_im = out_map(0, False)
    ob_re, ob_im = out_map(1, True)
    mout = jnp.stack([of_re, of_im, ob_re, ob_im], axis=1)

    def compact(m, rows):
        m = m.reshape(nblk, gb, m.shape[1], m.shape[2], -1)
        return jnp.transpose(m, (0, 2, 1, 3, 4)).reshape(nblk, rows, m.shape[-1]).astype(BF16)

    wc = compact(mend, t * gb * gi)
    tc = compact(toep, t * gb * gi)
    mc = compact(mout, 4 * gb * p)

    a16 = jnp.stack([ap_re[0][:, t], ap_im[0][:, t], ap_re[1][:, t], ap_im[1][:, t]], axis=1)
    a_blk = jnp.transpose(a16.reshape(nblk, gb, 4, p), (0, 2, 1, 3)).reshape(nblk, 4, gb * p)
    return wc, tc, mc, a_blk


def _s5_replication():
    def rep(nx, ny):
        r = np.zeros((nx, ny, nx, S5_GB, ny), np.float32)
        ix, iy = np.meshgrid(np.arange(nx), np.arange(ny), indexing='ij')
        r[ix, iy, ix, :, iy] = 1.0
        return jnp.asarray(r.reshape(nx * ny, nx * S5_GB * ny), dtype=BF16)
    return rep(4, SSM_STATE), rep(SSM_CHUNK, SSM_GROUP)


def kernel(x, c, ctx, c_ctx, w_mod, b_mod, w_in, ssm_lam_re, ssm_lam_im, ssm_log_dt, ssm_b_re, ssm_b_im,
           ssm_c_re, ssm_c_im, ssm_d, ssm_w_glu, ssm_b_glu, na_rpb, mla_q_norm, mla_w_uq, mla_kv_norm,
           mla_w_ukv, w_branch_a, w_branch_b, w_branch_c, w_out, ln_g, ln_b):
    b, nlat, d = x.shape
    nctx = ctx.shape[1]
    ltot = nlat + nctx
    depth = w_mod.shape[0]
    assert nlat % TOKEN_TILE == 0 and nctx % TOKEN_TILE == 0 and nlat % nctx == 0
    assert nlat % (NA_ROW_BLOCK * GRID_W) == 0 and b + 1 <= 8
    assert nlat % (S5_SUB * SSM_CHUNK) == 0 and nctx % (S5_SUB * SSM_CHUNK) == 0
    n_lat_tiles = nlat // TOKEN_TILE
    alpha = (2 * depth) ** 0.25
    fa_tq = FA_TQ if nlat % FA_TQ == 0 else nctx
    fa_tk = FA_TK if ltot % FA_TK == 0 else nctx

    cc = jnp.concatenate([c, c_ctx[None], jnp.zeros((8 - b - 1, d), F32)], axis=0)
    mod_all = _mod_call(cc, w_mod, b_mod)
    mod_lat = mod_all[:, :b].reshape(depth, b, 1, 3, d)
    mod_ctx = jnp.broadcast_to(mod_all[:, b].reshape(depth, 1, 1, 3, d), (depth, b, 1, 3, d))
    modt = jnp.concatenate([mod_lat, mod_ctx], axis=2)

    cos_t, sin_t = _rope_tables(nlat, nctx)

    w_pad = _prep_w_in(w_in)
    s5w = jax.vmap(_s5_mats)(ssm_lam_re, ssm_lam_im, ssm_log_dt, ssm_b_re, ssm_b_im, ssm_c_re, ssm_c_im)
    s5_wc, s5_tc, s5_mc, s5_a = (v.reshape((-1,) + v.shape[2:]) for v in s5w)
    rep_state, rep_out = _s5_replication()
    bias_all = _na_bias_tables(na_rpb)
    wq, wqs, wk, wks, wv = jax.vmap(_prep_mla)(mla_w_uq, mla_w_ukv)
    ones = _mla_ones_row()
    bf = lambda w: w.astype(BF16)
    row = lambda v: v.reshape(depth, 1, v.shape[-1])
    wglu, wa, wb, wc, wo = bf(ssm_w_glu), bf(w_branch_a), bf(w_branch_b), bf(w_branch_c), bf(w_out)
    bglu, dskip, lng, lnb = row(ssm_b_glu), row(ssm_d), row(ln_g), row(ln_b)
    gq, gkv = row(mla_q_norm), row(mla_kv_norm)

    x_lat, x_ctx, x_ctx_tile0 = x, ctx, 0
    for l in range(depth):
        n_out_tiles = n_lat_tiles if l == depth - 1 else ltot // TOKEN_TILE
        u, z, na, mc, g = _in_call(x_lat, x_ctx, x_ctx_tile0, ltot, modt, w_pad, n_lat_tiles, l)
        y = _s5_call(u, s5_wc, s5_tc, s5_mc, s5_a, rep_state, rep_out, nlat, nctx, l)
        yn_lat, yn_ctx = _na_call(na, bias_all, nlat, nctx, l), _na_ctx_call(na, nlat, nctx)
        q, k, vt = _mlap_call(mc, cos_t, sin_t, gq, gkv, wq, wqs, wk, wks, wv, ones, l)
        ym_lat = _fa_call(q, k, vt, nlat, 0, ltot, 0, fa_tq, fa_tk)
        ym_ctx = _fa_call(q, k, vt, nctx, nlat, nctx, nlat, nctx, nctx)
        xcat = _merge_call(y, u, z, g, yn_lat, yn_ctx, ym_lat, ym_ctx, x_lat, x_ctx, x_ctx_tile0, modt,
                           wglu, bglu, dskip, wa, wb, wc, wo, lng, lnb, n_lat_tiles, alpha, l, n_out_tiles)
        x_lat, x_ctx, x_ctx_tile0 = xcat, xcat, n_lat_tiles
    return xcat
```

```python
import functools
import math

import jax
import jax.numpy as jnp
import numpy as np
from jax import lax
from jax.experimental import pallas as pl
from jax.experimental.pallas import tpu as pltpu

F32 = jnp.float32
BF16 = jnp.bfloat16

GRID_W = 64
SSM_GROUP = 16
SSM_STATE = 64
SSM_CHUNK = 16
NA_HEADS = 8
NA_HEAD_DIM = 64
NA_WIN_ROWS = 8
NA_WIN_COLS = 16
MLA_HEADS = 8
MLA_Q_RANK = 256
MLA_KV_RANK = 128
MLA_NOPE = 64
MLA_ROPE = 32
MLA_V = 64
MLA_QK = MLA_NOPE + MLA_ROPE
ROPE_THETA = 10000.0
LN_EPS = 1e-5
RMS_EPS = 1e-6
NEG_INF = -1e30
LOG2E = math.log2(math.e)
LANES = 128
HALF = 64

TOKEN_TILE = 256
NA_ROW_BLOCK = 8
FA_TQ = 1024
FA_TK = 1408
VMEM_LIMIT = 56 * 1024 * 1024


def _sigmoid(x):
    return 1.0 / (1.0 + jnp.exp(-x))


def _dot(a, b):
    return jnp.dot(a, b, preferred_element_type=F32)


def _dot_nt(a, b):
    return lax.dot_general(a, b, (((1,), (1,)), ((), ())), preferred_element_type=F32)


def _params(*sem):
    return pltpu.CompilerParams(dimension_semantics=sem, vmem_limit_bytes=VMEM_LIMIT)


def _layer_spec(arr, layer, grid_rank):
    zeros = (0,) * (arr.ndim - 1)
    if grid_rank == 2:
        imap = lambda a, b: (layer,) + zeros
    else:
        imap = lambda a, b, c: (layer,) + zeros
    return pl.BlockSpec((None,) + arr.shape[1:], imap)


def _mod_kernel(cc_ref, w_ref, b_ref, o_ref):
    cc = cc_ref[...]
    s = (cc * _sigmoid(cc)).astype(BF16)
    o_ref[0] = _dot(s, w_ref[0].astype(BF16)) + b_ref[0]


def _mod_call(cc, w_mod, b_mod):
    depth, d, n = w_mod.shape
    tn = 1024
    return pl.pallas_call(
        _mod_kernel,
        out_shape=jax.ShapeDtypeStruct((depth, 8, n), F32),
        grid=(depth, n // tn),
        in_specs=[pl.BlockSpec((8, d), lambda l, j: (0, 0)),
                  pl.BlockSpec((1, d, tn), lambda l, j: (l, 0, j)),
                  pl.BlockSpec((1, 1, tn), lambda l, j: (l, 0, j))],
        out_specs=pl.BlockSpec((1, 8, tn), lambda l, j: (l, 0, j)),
        compiler_params=_params("parallel", "parallel"),
        name="adaln_mod",
    )(cc, w_mod, b_mod.reshape(depth, 1, n))


IN_WIDTH_PADDED = 7168
OUT_COLS = (512, 1536, 1536, 512, 3072)


def _two_source_specs(width, n_lat_tiles, ctx_tile0):
    tm = TOKEN_TILE
    lat = pl.BlockSpec((1, tm, width), lambda bi, i: (bi, jnp.minimum(i, n_lat_tiles - 1), 0))
    ctx = pl.BlockSpec((1, tm, width), lambda bi, i: (bi, ctx_tile0 + jnp.maximum(i - n_lat_tiles, 0), 0))
    return [lat, ctx]


def _pick_source(lat_ref, ctx_ref, n_lat_tiles):
    return jnp.where(pl.program_id(1) >= n_lat_tiles, ctx_ref[0], lat_ref[0])


def _in_kernel(xl_ref, xc_ref, mod_ref, w_ref, o_u, o_z, o_na, o_mc, o_g, *, n_lat_tiles):
    x = _pick_source(xl_ref, xc_ref, n_lat_tiles)
    mu = jnp.mean(x, axis=-1, keepdims=True)
    xc = x - mu
    var = jnp.mean(xc * xc, axis=-1, keepdims=True)
    mod = mod_ref[0, 0]
    h = (xc * lax.rsqrt(var + LN_EPS) * (1.0 + mod[1:2]) + mod[0:1]).astype(BF16)

    def mm(lo, hi):
        return _dot(h, w_ref[:, lo:hi])

    def silu(lo, hi):
        z = mm(lo, hi)
        return (z * _sigmoid(z)).astype(BF16)

    o_u[0] = mm(0, 512)
    o_z[0, :, 0:512] = silu(512, 1024)
    o_na[0] = mm(1024, 2560).astype(BF16)
    o_z[0, :, 512:1024] = silu(2560, 3072)
    o_mc[0] = mm(3072, 3584).astype(BF16)
    o_z[0, :, 1024:1536] = silu(3584, 4096)
    o_g[0] = _sigmoid(mm(4096, 7168)).astype(BF16)


def _in_call(x_lat, x_ctx, ctx_tile0, ltot, modt, w_pad, n_lat_tiles, layer):
    b, _, d = x_lat.shape
    tm = TOKEN_TILE
    tok = lambda w: pl.BlockSpec((1, tm, w), lambda bi, i: (bi, i, 0))
    dts = (F32, BF16, BF16, BF16, BF16)
    return pl.pallas_call(
        functools.partial(_in_kernel, n_lat_tiles=n_lat_tiles),
        out_shape=tuple(jax.ShapeDtypeStruct((b, ltot, w), dt) for w, dt in zip(OUT_COLS, dts)),
        grid=(b, ltot // tm),
        in_specs=_two_source_specs(d, n_lat_tiles, ctx_tile0) + [
            pl.BlockSpec((None, 1, 1, 3, d), lambda bi, i: (layer, bi, i // n_lat_tiles, 0, 0)),
            pl.BlockSpec((None,) + w_pad.shape[1:], lambda bi, i: (layer, 0, 0), pipeline_mode=pl.Buffered(1))],
        out_specs=tuple(tok(w) for w in OUT_COLS),
        compiler_params=_params("parallel", "parallel"),
        name="ln_in_proj",
    )(x_lat, x_ctx, modt, w_pad)


S5_SUB = 8
S5_GB = LANES // SSM_GROUP
S5_STATE_COLS = S5_GB * SSM_STATE
S5_EXPAND_CHUNK = 512


def _s5_gather_chunks(u_ref, nch):
    cols = [u_ref[0, pl.ds(t, nch, stride=SSM_CHUNK), :].astype(BF16) for t in range(SSM_CHUNK)]
    return jnp.concatenate(cols, axis=1)


def _s5_expand(c_ref, rep_ref, w_scr, row_unit, col_unit):
    comp = c_ref[0]
    nrow, ncol = w_scr.shape
    ch = S5_EXPAND_CHUNK
    rg = (lax.broadcasted_iota(jnp.int32, (nrow, ch), 0) // row_unit) % S5_GB
    for j in range(ncol // ch):
        cg = ((lax.broadcasted_iota(jnp.int32, (nrow, ch), 1) + j * ch) // col_unit) % S5_GB
        full = _dot(comp, rep_ref[:, j * ch:(j + 1) * ch])
        w_scr[:, j * ch:(j + 1) * ch] = jnp.where(rg == cg, full, 0.0).astype(BF16)


def _s5_state_kernel(u_ref, wc_ref, rep_ref, a_ref, s_ref, we_scr, e_scr, s_scr, *, nlat, nctx):
    ntot = nlat + nctx
    w = S5_STATE_COLS

    @pl.when(pl.program_id(1) == 0)
    def _():
        _s5_expand(wc_ref, rep_ref, we_scr, SSM_GROUP, SSM_STATE)

    e_scr[...] = _dot(_s5_gather_chunks(u_ref, ntot), we_scr[...])
    a = a_ref[0]
    arf, aif, arb, aib = a[0:1], a[1:2], a[2:3], a[3:4]
    ntile, nctx_t, nlat_t = ntot // S5_SUB, nctx // S5_SUB, nlat // S5_SUB

    def sweep(er, ei, sr, si, ar, ai, order):
        rows_r, rows_i = [None] * S5_SUB, [None] * S5_SUB
        for k in order:
            rows_r[k], rows_i[k] = sr, si
            sr, si = ar * sr - ai * si + er[k:k + 1], ar * si + ai * sr + ei[k:k + 1]
        return jnp.concatenate(rows_r, axis=0), jnp.concatenate(rows_i, axis=0), sr, si

    def body(i, carry):
        srf, sif, srb, sib = carry
        rf = pl.multiple_of(jnp.where(i < nctx_t, nlat_t + i, i - nctx_t) * S5_SUB, S5_SUB)
        rb = pl.multiple_of((ntile - 1 - i) * S5_SUB, S5_SUB)
        pfr, pfi, srf, sif = sweep(e_scr[pl.ds(rf, S5_SUB), 0:w], e_scr[pl.ds(rf, S5_SUB), w:2 * w],
                                   srf, sif, arf, aif, range(S5_SUB))
        pbr, pbi, srb, sib = sweep(e_scr[pl.ds(rb, S5_SUB), 2 * w:3 * w], e_scr[pl.ds(rb, S5_SUB), 3 * w:4 * w],
                                   srb, sib, arb, aib, range(S5_SUB - 1, -1, -1))
        s_scr[pl.ds(rf, S5_SUB), 0:w] = pfr
        s_scr[pl.ds(rf, S5_SUB), w:2 * w] = pfi
        s_scr[pl.ds(rb, S5_SUB), 2 * w:3 * w] = pbr
        s_scr[pl.ds(rb, S5_SUB), 3 * w:4 * w] = pbi
        return srf, sif, srb, sib

    zero = jnp.zeros((1, w), F32)
    lax.fori_loop(0, ntile, body, (zero, zero, zero, zero))
    s_ref[0, 0] = s_scr[...].astype(BF16)


def _s5_out_kernel(u_ref, s_ref, tc_ref, mc_ref, rep_ref, y_ref, wy_scr, mo_scr, *, nch):
    @pl.when(pl.program_id(1) == 0)
    def _():
        _s5_expand(tc_ref, rep_ref, wy_scr, SSM_GROUP, SSM_GROUP)
        _s5_expand(mc_ref, rep_ref, mo_scr, SSM_STATE, SSM_GROUP)

    lhs = _s5_gather_chunks(u_ref, nch)
    s = s_ref[0, 0]
    half = SSM_CHUNK // 2
    for hf in range(2):
        cols = slice(hf * half * LANES, (hf + 1) * half * LANES)
        y = _dot(lhs, wy_scr[:, cols]) + _dot(s, mo_scr[:, cols])
        for t in range(half):
            y_ref[0, pl.ds(hf * half + t, nch, stride=SSM_CHUNK), :] = y[:, t * LANES:(t + 1) * LANES]


def _s5_call(u, wc, tc, mc, a, rep_state, rep_out, nlat, nctx, layer):
    b, ltot, width = u.shape
    nblk = width // LANES
    nch = ltot // SSM_CHUNK
    kdim = SSM_CHUNK * LANES
    scols = 4 * S5_STATE_COLS
    ublk = pl.BlockSpec((1, ltot, LANES), lambda j, bi: (bi, 0, j))
    sblk = pl.BlockSpec((1, 1, nch, scols), lambda j, bi: (bi, j, 0, 0))
    cblk = lambda c: pl.BlockSpec((1,) + c.shape[1:], lambda j, bi: (layer * nblk + j, 0, 0))
    rblk = lambda r: pl.BlockSpec(r.shape, lambda j, bi: (0, 0))
    sem = ("parallel", "arbitrary")
    states = pl.pallas_call(
        functools.partial(_s5_state_kernel, nlat=nlat // SSM_CHUNK, nctx=nctx // SSM_CHUNK),
        out_shape=jax.ShapeDtypeStruct((b, nblk, nch, scols), BF16),
        grid=(nblk, b),
        in_specs=[ublk, cblk(wc), rblk(rep_state), cblk(a)],
        out_specs=sblk,
        scratch_shapes=[pltpu.VMEM((kdim, scols), BF16), pltpu.VMEM((nch, scols), F32),
                        pltpu.VMEM((nch, scols), F32)],
        compiler_params=_params(*sem),
        name="s5_states",
    )(u, wc, rep_state, a)
    return pl.pallas_call(
        functools.partial(_s5_out_kernel, nch=nch),
        out_shape=jax.ShapeDtypeStruct((b, ltot, width), F32),
        grid=(nblk, b),
        in_specs=[ublk, sblk, cblk(tc), cblk(mc), rblk(rep_out)],
        out_specs=ublk,
        scratch_shapes=[pltpu.VMEM((kdim, kdim), BF16), pltpu.VMEM((scols, kdim), BF16)],
        compiler_params=_params(*sem),
        name="s5_outputs",
    )(u, states, tc, mc, rep_out)


NA_UNION = NA_ROW_BLOCK + NA_WIN_ROWS - 1


def _na_kernel(q_ref, k_ref, v_ref, pt_ref, o_ref, bias_scr, sw_scr, sc_scr, *, nrows, nlat, nctx):
    i = pl.program_id(2)
    nblk = nrows // NA_ROW_BLOCK
    r0 = i * NA_ROW_BLOCK
    half_win = NA_WIN_ROWS // 2
    ustart = jnp.clip(r0 - half_win, 0, nrows - NA_UNION)
    nkl = NA_UNION * GRID_W

    @pl.when((i <= 1) | (i == nblk - 1))
    def _():
        lane = lax.broadcasted_iota(jnp.int32, (1, LANES), 1)
        for rp in range(NA_ROW_BLOCK // 2):
            ra = r0 + 2 * rp
            sa = jnp.clip(ra - half_win, 0, nrows - NA_WIN_ROWS)
            sb = jnp.clip(ra + 1 - half_win, 0, nrows - NA_WIN_ROWS)
            for u in range(NA_UNION):
                kr = ustart + u
                pen_a = jnp.where((kr >= sa) & (kr < sa + NA_WIN_ROWS), 0.0, NEG_INF)
                pen_b = jnp.where((kr >= sb) & (kr < sb + NA_WIN_ROWS), 0.0, NEG_INF)
                pen = jnp.where(lane < HALF, pen_a, pen_b)
                d = jnp.clip(kr - ra + NA_WIN_ROWS - 1, 1, 2 * NA_WIN_ROWS - 2)
                for hh in range(2):
                    bias_scr[hh, u * GRID_W:(u + 1) * GRID_W, rp * LANES:(rp + 1) * LANES] = pt_ref[hh, d] + pen

    off = pl.multiple_of(ustart * GRID_W, GRID_W)
    kw = k_ref[0, pl.ds(off, nkl), :]
    kc = k_ref[0, nlat:nlat + nctx, :]
    vwt = v_ref[0, pl.ds(off, nkl), :].T
    vct = v_ref[0, nlat:nlat + nctx, :].T
    q = q_ref[0]
    lane = lax.broadcasted_iota(jnp.int32, q.shape, 1)
    for hh in range(2):
        qm = jnp.where(lane < HALF if hh == 0 else lane >= HALF, q, jnp.zeros_like(q))
        sw_scr[hh] = _dot_nt(kw, qm)
        sc_scr[hh] = _dot_nt(kc, qm)
    outs = []
    for hh in range(2):
        sw = sw_scr[hh] + bias_scr[hh]
        sc = sc_scr[hh]
        m = jnp.maximum(jnp.max(sw, axis=0, keepdims=True), jnp.max(sc, axis=0, keepdims=True))
        pw = jnp.exp2(sw - m)
        pc = jnp.exp2(sc - m)
        den = jnp.sum(pw, axis=0, keepdims=True) + jnp.sum(pc, axis=0, keepdims=True)
        o = _dot(vwt, pw.astype(BF16)) + _dot(vct, pc.astype(BF16))
        outs.append(o / den)
    ot = jnp.concatenate([outs[0][0:HALF], outs[1][HALF:LANES]], axis=0)
    o_ref[0] = ot.T.astype(BF16)


def _na_call(na, pt_all, nlat, nctx, layer):
    b, ltot, _ = na.shape
    nrows = nlat // GRID_W
    rb = NA_ROW_BLOCK
    npair = NA_HEADS // 2
    kern = functools.partial(_na_kernel, nrows=nrows, nlat=nlat, nctx=nctx)
    return pl.pallas_call(
        kern,
        out_shape=jax.ShapeDtypeStruct((b, nlat, NA_HEADS * NA_HEAD_DIM), BF16),
        grid=(b, npair, nrows // rb),
        in_specs=[pl.BlockSpec((1, rb * GRID_W, LANES), lambda bi, p, i: (bi, i, p)),
                  pl.BlockSpec((1, ltot, LANES), lambda bi, p, i: (bi, 0, npair + p)),
                  pl.BlockSpec((1, ltot, LANES), lambda bi, p, i: (bi, 0, 2 * npair + p)),
                  pl.BlockSpec((2, 2 * NA_WIN_ROWS - 1, GRID_W, LANES),
                               lambda bi, p, i: (layer * npair + p, 0, 0, 0))],
        out_specs=pl.BlockSpec((1, rb * GRID_W, LANES), lambda bi, p, i: (bi, i, p)),
        scratch_shapes=[pltpu.VMEM((2, NA_UNION * GRID_W, rb * GRID_W), F32),
                        pltpu.VMEM((2, NA_UNION * GRID_W, rb * GRID_W), F32),
                        pltpu.VMEM((2, nctx, rb * GRID_W), F32)],
        compiler_params=_params("parallel", "parallel", "arbitrary"),
        name="na_latent",
    )(na, na, na, pt_all)


def _na_ctx_kernel(q_ref, k_ref, v_ref, o_ref):
    n = q_ref.shape[1]
    lane = lax.broadcasted_iota(jnp.int32, (n, LANES), 1)
    lo = lane < HALF
    hi = lane >= HALF
    q = q_ref[0]
    k = k_ref[0]
    v = v_ref[0]
    outs = []
    for hh in range(2):
        qm = jnp.where(lo if hh == 0 else hi, q, jnp.zeros_like(q))
        s = _dot_nt(qm, k)
        p = jnp.exp2(s - jnp.max(s, axis=-1, keepdims=True))
        outs.append(_dot(p.astype(BF16), v) / jnp.sum(p, axis=-1, keepdims=True))
    o_ref[0] = jnp.where(lo, outs[0], outs[1]).astype(BF16)


def _na_ctx_call(na, nlat, nctx):
    b = na.shape[0]
    npair = NA_HEADS // 2
    rblk = nlat // nctx
    return pl.pallas_call(
        _na_ctx_kernel,
        out_shape=jax.ShapeDtypeStruct((b, nctx, NA_HEADS * NA_HEAD_DIM), BF16),
        grid=(b, npair),
        in_specs=[pl.BlockSpec((1, nctx, LANES), lambda bi, p: (bi, rblk, p)),
                  pl.BlockSpec((1, nctx, LANES), lambda bi, p: (bi, rblk, npair + p)),
                  pl.BlockSpec((1, nctx, LANES), lambda bi, p: (bi, rblk, 2 * npair + p))],
        out_specs=pl.BlockSpec((1, nctx, LANES), lambda bi, p: (bi, 0, p)),
        compiler_params=_params("parallel", "parallel"),
        name="na_context",
    )(na, na, na)


def _mlap_kernel(mc_ref, cos_ref, sin_ref, gq_ref, gkv_ref, wq_ref, wqs_ref, wk_ref, wks_ref,
                 wv_ref, ones_ref, q_ref, k_ref, vt_ref):
    mc = mc_ref[0]
    cq = mc[:, 0:MLA_Q_RANK].astype(F32)
    ckv = mc[:, MLA_Q_RANK:MLA_Q_RANK + MLA_KV_RANK].astype(F32)
    kr = mc[:, MLA_Q_RANK + MLA_KV_RANK:]
    nq = (cq * lax.rsqrt(jnp.mean(cq * cq, axis=-1, keepdims=True) + RMS_EPS) * gq_ref[...]).astype(BF16)
    nkv = (ckv * lax.rsqrt(jnp.mean(ckv * ckv, axis=-1, keepdims=True) + RMS_EPS) * gkv_ref[...]).astype(BF16)
    cos = cos_ref[...]
    sin = sin_ref[...]
    q1 = _dot(nq, wq_ref[...])
    q2 = _dot(nq, wqs_ref[...])
    k1 = _dot(nkv, wk_ref[0:MLA_KV_RANK, :]) + _dot(kr, wk_ref[MLA_KV_RANK:, :])
    k2 = _dot(kr, wks_ref[...])
    v = _dot(nkv, wv_ref[...]) + ones_ref[...]
    for h in range(MLA_HEADS):
        sl = slice(h * LANES, (h + 1) * LANES)
        q_ref[0, h] = (q1[:, sl] * cos + q2[:, sl] * sin).astype(BF16)
        k_ref[0, h] = (k1[:, sl] * cos + k2[:, sl] * sin).astype(BF16)
        vt_ref[0, h] = v[:, sl].T.astype(BF16)


def _mlap_call(mc, cos_t, sin_t, gq, gkv, wq, wqs, wk, wks, wv, ones, layer):
    b, ltot, _ = mc.shape
    tm = TOKEN_TILE
    lay = lambda a: _layer_spec(a, layer, 2)
    hm = jax.ShapeDtypeStruct((b, MLA_HEADS, ltot, LANES), BF16)
    hspec = pl.BlockSpec((1, MLA_HEADS, tm, LANES), lambda bi, i: (bi, 0, i, 0))
    return pl.pallas_call(
        _mlap_kernel,
        out_shape=(hm, hm, jax.ShapeDtypeStruct((b, MLA_HEADS, LANES, ltot), BF16)),
        grid=(b, ltot // tm),
        in_specs=[pl.BlockSpec((1, tm, mc.shape[2]), lambda bi, i: (bi, i, 0)),
                  pl.BlockSpec((tm, LANES), lambda bi, i: (i, 0)),
                  pl.BlockSpec((tm, LANES), lambda bi, i: (i, 0)),
                  lay(gq), lay(gkv), lay(wq), lay(wqs), lay(wk), lay(wks), lay(wv),
                  pl.BlockSpec(ones.shape, lambda bi, i: (0, 0))],
        out_specs=(hspec, hspec, pl.BlockSpec((1, MLA_HEADS, LANES, tm), lambda bi, i: (bi, 0, 0, i))),
        compiler_params=_params("parallel", "parallel"),
        name="mla_proj",
    )(mc, cos_t, sin_t, gq, gkv, wq, wqs, wk, wks, wv, ones)


def _fa_kernel(q_ref, k_ref, vt_ref, o_ref, sa_scr, sb_scr, *, tq, tk, nk):
    qs = [q_ref[0, hh] for hh in range(2)]

    def scores(c, s_scr):
        off = pl.multiple_of(c * tk, tk)
        for hh in range(2):
            s_scr[hh] = _dot_nt(k_ref[0, hh, pl.ds(off, tk), :], qs[hh])

    def softmax_pv(c, s_scr, carry):
        off = pl.multiple_of(c * tk, tk)
        new = []
        for hh in range(2):
            m, acc = carry[2 * hh], carry[2 * hh + 1]
            s = s_scr[hh]
            mn = jnp.maximum(m, jnp.max(s, axis=0, keepdims=True))
            p = jnp.exp2(s - mn).astype(BF16)
            new += [mn, acc * jnp.exp2(m - mn) + _dot(vt_ref[0, hh, :, pl.ds(off, tk)], p)]
        return tuple(new)

    def pair(i, carry):
        scores(2 * i + 1, sb_scr)
        carry = softmax_pv(2 * i, sa_scr, carry)
        scores(2 * i + 2, sa_scr)
        return softmax_pv(2 * i + 1, sb_scr, carry)

    m0 = jnp.full((1, tq), NEG_INF, F32)
    acc0 = jnp.zeros((LANES, tq), F32)
    res = (m0, acc0, m0, acc0)
    scores(0, sa_scr)
    npairs = (nk - 1) // 2
    res = lax.fori_loop(0, npairs, pair, res)
    c = 2 * npairs
    if nk - c == 2:
        scores(c + 1, sb_scr)
        res = softmax_pv(c, sa_scr, res)
        res = softmax_pv(c + 1, sb_scr, res)
    else:
        res = softmax_pv(c, sa_scr, res)
    o0 = res[1][0:HALF] / res[1][HALF:HALF + 1]
    o1 = res[3][HALF:LANES] / res[3][0:1]
    o_ref[0] = jnp.concatenate([o0, o1], axis=0).T.astype(BF16)


def _fa_call(q, k, vt, nq_rows, q_row0, lk, k_row0, tq, tk):
    b, h, _, _ = q.shape
    npair = h // 2
    kern = functools.partial(_fa_kernel, tq=tq, tk=tk, nk=lk // tk)
    qb0 = q_row0 // tq
    kb0 = k_row0 // lk
    return pl.pallas_call(
        kern,
        out_shape=jax.ShapeDtypeStruct((b, nq_rows, h * HALF), BF16),
        grid=(b, npair, nq_rows // tq),
        in_specs=[pl.BlockSpec((1, 2, tq, LANES), lambda bi, p, i: (bi, p, qb0 + i, 0)),
                  pl.BlockSpec((1, 2, lk, LANES), lambda bi, p, i: (bi, p, kb0, 0)),
                  pl.BlockSpec((1, 2, LANES, lk), lambda bi, p, i: (bi, p, 0, kb0))],
        out_specs=pl.BlockSpec((1, tq, LANES), lambda bi, p, i: (bi, i, p)),
        scratch_shapes=[pltpu.VMEM((2, tk, tq), F32), pltpu.VMEM((2, tk, tq), F32)],
        compiler_params=_params("parallel", "parallel", "parallel"),
        name="mla_attn",
    )(q, k, vt)


def _gelu_tanh(x):
    return 0.5 * x * (1.0 + jnp.tanh(math.sqrt(2.0 / math.pi) * (x + 0.044715 * (x * x * x))))


def _merge_kernel(y_ref, u_ref, z_ref, g_ref, ynl_ref, ync_ref, yml_ref, ymc_ref, xl_ref, xc_ref, mod_ref,
                  wglu_ref, bglu_ref, d_ref, wa_ref, wb_ref, wc_ref, wo_ref, lng_ref, lnb_ref, o_ref,
                  *, alpha, d_model, width, n_lat_tiles):
    ya = _gelu_tanh(y_ref[0] + d_ref[...] * u_ref[0])
    ya = ya * _sigmoid(_dot(ya.astype(BF16), wglu_ref[...]) + bglu_ref[...])
    z = z_ref[0]
    g = g_ref[0]
    yn = _pick_source(ynl_ref, ync_ref, n_lat_tiles)
    ym = _pick_source(yml_ref, ymc_ref, n_lat_tiles)
    ba = _dot((ya * z[:, 0:width].astype(F32)).astype(BF16), wa_ref[...])
    bn = _dot(yn * z[:, width:2 * width], wb_ref[...])
    bm = _dot(ym * z[:, 2 * width:3 * width], wc_ref[...])
    m = (g[:, 0:d_model].astype(F32) * ba + g[:, d_model:2 * d_model].astype(F32) * bn
         + g[:, 2 * d_model:3 * d_model].astype(F32) * bm)
    out = _dot(m.astype(BF16), wo_ref[...])
    t = alpha * _pick_source(xl_ref, xc_ref, n_lat_tiles) + mod_ref[0, 0][2:3] * out
    mu = jnp.mean(t, axis=-1, keepdims=True)
    tc = t - mu
    var = jnp.mean(tc * tc, axis=-1, keepdims=True)
    o_ref[0] = tc * lax.rsqrt(var + LN_EPS) * lng_ref[...] + lnb_ref[...]


def _merge_call(y, u, z, g, yn_lat, yn_ctx, ym_lat, ym_ctx, x_lat, x_ctx, x_ctx_tile0, modt,
                wglu, bglu, dskip, wa, wb, wc, wo, lng, lnb, n_lat_tiles, alpha, layer, n_out_tiles):
    b, _, d = x_lat.shape
    width = u.shape[2]
    tm = TOKEN_TILE
    tok = lambda a: pl.BlockSpec((1, tm, a.shape[2]), lambda bi, i: (bi, i, 0))
    lay = lambda a: _layer_spec(a, layer, 2)
    kern = functools.partial(_merge_kernel, alpha=alpha, d_model=d, width=width, n_lat_tiles=n_lat_tiles)
    return pl.pallas_call(
        kern,
        out_shape=jax.ShapeDtypeStruct((b, n_out_tiles * tm, d), F32),
        grid=(b, n_out_tiles),
        in_specs=[tok(y), tok(u), tok(z), tok(g)]
        + _two_source_specs(width, n_lat_tiles, 0) + _two_source_specs(width, n_lat_tiles, 0)
        + _two_source_specs(d, n_lat_tiles, x_ctx_tile0)
        + [pl.BlockSpec((None, 1, 1, 3, d), lambda bi, i: (layer, bi, i // n_lat_tiles, 0, 0)),
           lay(wglu), lay(bglu), lay(dskip), lay(wa), lay(wb), lay(wc), lay(wo), lay(lng), lay(lnb)],
        out_specs=pl.BlockSpec((1, tm, d), lambda bi, i: (bi, i, 0)),
        compiler_params=_params("parallel", "parallel"),
        name="merge_deepnorm",
    )(y, u, z, g, yn_lat, yn_ctx, ym_lat, ym_ctx, x_lat, x_ctx, modt,
      wglu, bglu, dskip, wa, wb, wc, wo, lng, lnb)


IN_PAD_AT = 6 * 512 + MLA_Q_RANK + MLA_KV_RANK + MLA_ROPE


def _wprep_kernel(w_ref, o_ref):
    w = w_ref[0]
    pad_end = IN_PAD_AT + LANES - MLA_ROPE
    o_ref[0, :, 0:1024] = w[:, 0:1024].astype(BF16)
    o_ref[0, :, 1024:1536] = (w[:, 1024:1536] * (NA_HEAD_DIM ** -0.5 * LOG2E)).astype(BF16)
    o_ref[0, :, 1536:IN_PAD_AT] = w[:, 1536:IN_PAD_AT].astype(BF16)
    o_ref[0, :, IN_PAD_AT:pad_end] = jnp.zeros((w.shape[0], pad_end - IN_PAD_AT), BF16)
    o_ref[0, :, pad_end:IN_WIDTH_PADDED] = w[:, IN_PAD_AT:].astype(BF16)


def _prep_w_in(w_in):
    depth, d, n = w_in.shape
    tr = TOKEN_TILE
    return pl.pallas_call(
        _wprep_kernel,
        out_shape=jax.ShapeDtypeStruct((depth, d, IN_WIDTH_PADDED), BF16),
        grid=(depth, d // tr),
        in_specs=[pl.BlockSpec((1, tr, n), lambda l, i: (l, i, 0))],
        out_specs=pl.BlockSpec((1, tr, IN_WIDTH_PADDED), lambda l, i: (l, i, 0)),
        compiler_params=_params("parallel", "parallel"),
        name="w_in_prep",
    )(w_in)


_ROPE_SWAP = np.concatenate([np.arange(8, 16), np.arange(0, 8), np.arange(24, 32), np.arange(16, 24)])


def _prep_mla(w_uq, w_ukv):
    h = MLA_HEADS
    scale = MLA_QK ** -0.5 * LOG2E
    wq = w_uq.reshape(MLA_Q_RANK, h, MLA_QK) * scale
    zq = jnp.zeros((MLA_Q_RANK, h, LANES - MLA_QK), F32)
    nope, pe = wq[:, :, :MLA_NOPE], wq[:, :, MLA_NOPE:]
    wq_main = jnp.concatenate([nope, pe, zq], axis=2).reshape(MLA_Q_RANK, h * LANES)
    wq_swap = jnp.concatenate([jnp.zeros_like(nope), pe[:, :, _ROPE_SWAP], zq], axis=2).reshape(MLA_Q_RANK, h * LANES)

    wkv = w_ukv.reshape(MLA_KV_RANK, h, MLA_NOPE + MLA_V)
    knope, wv = wkv[:, :, :MLA_NOPE], wkv[:, :, MLA_NOPE:]
    zk = jnp.zeros((MLA_KV_RANK, h, LANES - MLA_NOPE), F32)
    wk_top = jnp.concatenate([knope, zk], axis=2).reshape(MLA_KV_RANK, h * LANES)
    eye = np.zeros((LANES, LANES), np.float32)
    eye[np.arange(MLA_ROPE), MLA_NOPE + np.arange(MLA_ROPE)] = 1.0
    eye_sw = np.zeros((LANES, LANES), np.float32)
    eye_sw[_ROPE_SWAP, MLA_NOPE + np.arange(MLA_ROPE)] = 1.0
    wk_bot = jnp.asarray(np.tile(eye, (1, h)))
    wk_swap = jnp.asarray(np.tile(eye_sw, (1, h)))
    wk_main = jnp.concatenate([wk_top, wk_bot], axis=0)

    zv = jnp.zeros_like(wv)
    even = jnp.concatenate([wv, zv], axis=2)
    odd = jnp.concatenate([zv, wv], axis=2)
    is_even = (np.arange(h) % 2 == 0)[None, :, None]
    wv_full = jnp.where(is_even, even, odd).reshape(MLA_KV_RANK, h * LANES)
    return (wq_main.astype(BF16), wq_swap.astype(BF16), wk_main.astype(BF16), wk_swap.astype(BF16),
            wv_full.astype(BF16))


def _mla_ones_row():
    ones = np.zeros((1, MLA_HEADS, LANES), np.float32)
    ones[0, 0::2, MLA_V] = 1.0
    ones[0, 1::2, 0] = 1.0
    return jnp.asarray(ones.reshape(1, MLA_HEADS * LANES))


def _rope_tables(nlat, nctx):
    t = jnp.arange(nlat, dtype=jnp.int32)
    row = (t // GRID_W).astype(F32)
    col = (t % GRID_W).astype(F32)
    half = MLA_ROPE // 2
    inv = 1.0 / (ROPE_THETA ** (jnp.arange(0, half, 2, dtype=F32) / half))
    ar = row[:, None] * inv[None, :]
    ac = col[:, None] * inv[None, :]
    one = jnp.ones((nlat, MLA_NOPE), F32)
    tail = jnp.ones((nlat, LANES - MLA_QK), F32)
    cos_t = jnp.concatenate([one, jnp.cos(ar), jnp.cos(ar), jnp.cos(ac), jnp.cos(ac), tail], axis=1)
    sin_t = jnp.concatenate([0 * one, -jnp.sin(ar), jnp.sin(ar), -jnp.sin(ac), jnp.sin(ac), 0 * tail], axis=1)
    cos_t = jnp.concatenate([cos_t, jnp.ones((nctx, LANES), F32)], axis=0)
    sin_t = jnp.concatenate([sin_t, jnp.zeros((nctx, LANES), F32)], axis=0)
    return cos_t, sin_t


def _na_bias_tables(rpb):
    depth, h = rpb.shape[:2]
    wc = NA_WIN_COLS
    col = np.arange(GRID_W)
    col_start = np.clip(col - wc // 2, 0, GRID_W - wc)
    in_win = (col[None, :] >= col_start[:, None]) & (col[None, :] < col_start[:, None] + wc)
    idx_c = np.clip(col[None, :] - col[:, None], -(wc - 1), wc - 1) + wc - 1
    oh_c = (idx_c.T[:, :, None] == np.arange(2 * wc - 1)).astype(np.float32)
    bt = jnp.einsum('lhdc,kqc->lhdkq', rpb, oh_c, precision=lax.Precision.HIGHEST)
    bt = jnp.where(in_win.T, bt * LOG2E, NEG_INF)
    pt = jnp.concatenate([bt, jnp.roll(bt, 1, axis=2)], axis=-1)
    return pt.reshape(depth * h, 2 * NA_WIN_ROWS - 1, GRID_W, 2 * GRID_W).astype(F32)


def _s5_mats(lam_re, lam_im, log_dt, b_re, b_im, c_re, c_im):
    t = SSM_CHUNK
    g, p = lam_re.shape[1], lam_re.shape[2]
    gi = b_re.shape[-1]
    gb = LANES // gi
    nblk = g // gb
    dt = jnp.exp(log_dt.astype(F32))[..., None]
    lr, li = lam_re.astype(F32), lam_im.astype(F32)
    mag = jnp.exp(lr * dt)
    ab_re, ab_im = mag * jnp.cos(li * dt), mag * jnp.sin(li * dt)
    nr = ab_re - 1.0
    den = lr * lr + li * li
    fr = (nr * lr + ab_im * li) / den
    fi = (ab_im * lr - nr * li) / den
    bb_re = fr[..., None] * b_re - fi[..., None] * b_im
    bb_im = fr[..., None] * b_im + fi[..., None] * b_re
    j = jnp.arange(t + 1, dtype=F32)[None, None, :, None]
    pmag = jnp.exp(lr[:, :, None, :] * dt[:, :, None, :] * j)
    ang = li[:, :, None, :] * dt[:, :, None, :] * j
    ap_re, ap_im = pmag * jnp.cos(ang), pmag * jnp.sin(ang)

    ca_re = c_re[:, :, None] * ap_re[:, :, :, None, :] - c_im[:, :, None] * ap_im[:, :, :, None, :]
    ca_im = c_re[:, :, None] * ap_im[:, :, :, None, :] + c_im[:, :, None] * ap_re[:, :, :, None, :]
    bt_re = jnp.swapaxes(bb_re, 2, 3)[:, :, None, None]
    bt_im = jnp.swapaxes(bb_im, 2, 3)[:, :, None, None]
    kmat = jnp.sum(ca_re[:, :, :, :, None, :] * bt_re - ca_im[:, :, :, :, None, :] * bt_im, axis=-1)
    seq = jnp.concatenate([jnp.flip(kmat[1][:, 1:t], axis=1), (kmat[0][:, 0:1] + kmat[1][:, 0:1]), kmat[0][:, 1:t]],
                          axis=1)
    toep = jnp.stack([seq[:, t - 1 - k:2 * t - 1 - k] for k in range(t)], axis=1)
    toep = jnp.transpose(toep, (0, 1, 4, 2, 3))

    def powers(v, d, lo, reverse):
        v = v[d][:, lo:lo + t]
        return jnp.flip(v, axis=1) if reverse else v

    def end_map(d, reverse):
        are, aim = powers(ap_re, d, 0, reverse), powers(ap_im, d, 0, reverse)
        bre, bim = jnp.transpose(bb_re[d], (0, 2, 1)), jnp.transpose(bb_im[d], (0, 2, 1))
        mre = are[:, :, None, :] * bre[:, None] - aim[:, :, None, :] * bim[:, None]
        mim = are[:, :, None, :] * bim[:, None] + aim[:, :, None, :] * bre[:, None]
        return mre, mim

    ef_re, ef_im = end_map(0, True)
    eb_re, eb_im = end_map(1, False)
    mend = jnp.stack([ef_re, ef_im, eb_re, eb_im], axis=3)

    def out_map(d, reverse):
        cre = jnp.transpose(powers(ca_re, d, 1, reverse), (0, 3, 1, 2))
        cim = jnp.transpose(powers(ca_im, d, 1, reverse), (0, 3, 1, 2))
        return cre, -cim

    of_re, of_im = out_map(0, False)
    ob_re, ob_im = out_map(1, True)
    mout = jnp.stack([of_re, of_im, ob_re, ob_im], axis=1)

    def compact(m, rows):
        m = m.reshape(nblk, gb, m.shape[1], m.shape[2], -1)
        return jnp.transpose(m, (0, 2, 1, 3, 4)).reshape(nblk, rows, m.shape[-1]).astype(BF16)

    wc = compact(mend, t * gb * gi)
    tc = compact(toep, t * gb * gi)
    mc = compact(mout, 4 * gb * p)

    a16 = jnp.stack([ap_re[0][:, t], ap_im[0][:, t], ap_re[1][:, t], ap_im[1][:, t]], axis=1)
    a_blk = jnp.transpose(a16.reshape(nblk, gb, 4, p), (0, 2, 1, 3)).reshape(nblk, 4, gb * p)
    return wc, tc, mc, a_blk


def _s5_replication():
    def rep(nx, ny):
        r = np.zeros((nx, ny, nx, S5_GB, ny), np.float32)
        ix, iy = np.meshgrid(np.arange(nx), np.arange(ny), indexing='ij')
        r[ix, iy, ix, :, iy] = 1.0
        return jnp.asarray(r.reshape(nx * ny, nx * S5_GB * ny), dtype=BF16)
    return rep(4, SSM_STATE), rep(SSM_CHUNK, SSM_GROUP)


def kernel(x, c, ctx, c_ctx, w_mod, b_mod, w_in, ssm_lam_re, ssm_lam_im, ssm_log_dt, ssm_b_re, ssm_b_im,
           ssm_c_re, ssm_c_im, ssm_d, ssm_w_glu, ssm_b_glu, na_rpb, mla_q_norm, mla_w_uq, mla_kv_norm,
           mla_w_ukv, w_branch_a, w_branch_b, w_branch_c, w_out, ln_g, ln_b):
    b, nlat, d = x.shape
    nctx = ctx.shape[1]
    ltot = nlat + nctx
    depth = w_mod.shape[0]
    assert nlat % TOKEN_TILE == 0 and nctx % TOKEN_TILE == 0 and nlat % nctx == 0
    assert nlat % (NA_ROW_BLOCK * GRID_W) == 0 and b + 1 <= 8
    assert nlat % (S5_SUB * SSM_CHUNK) == 0 and nctx % (S5_SUB * SSM_CHUNK) == 0
    n_lat_tiles = nlat // TOKEN_TILE
    alpha = (2 * depth) ** 0.25
    fa_tq = FA_TQ if nlat % FA_TQ == 0 else nctx
    fa_tk = FA_TK if ltot % FA_TK == 0 else nctx

    cc = jnp.concatenate([c, c_ctx[None], jnp.zeros((8 - b - 1, d), F32)], axis=0)
    mod_all = _mod_call(cc, w_mod, b_mod)
    mod_lat = mod_all[:, :b].reshape(depth, b, 1, 3, d)
    mod_ctx = jnp.broadcast_to(mod_all[:, b].reshape(depth, 1, 1, 3, d), (depth, b, 1, 3, d))
    modt = jnp.concatenate([mod_lat, mod_ctx], axis=2)

    cos_t, sin_t = _rope_tables(nlat, nctx)

    w_pad = _prep_w_in(w_in)
    s5w = jax.vmap(_s5_mats)(ssm_lam_re, ssm_lam_im, ssm_log_dt, ssm_b_re, ssm_b_im, ssm_c_re, ssm_c_im)
    s5_wc, s5_tc, s5_mc, s5_a = (v.reshape((-1,) + v.shape[2:]) for v in s5w)
    rep_state, rep_out = _s5_replication()
    bias_all = _na_bias_tables(na_rpb)
    wq, wqs, wk, wks, wv = jax.vmap(_prep_mla)(mla_w_uq, mla_w_ukv)
    ones = _mla_ones_row()
    bf = lambda w: w.astype(BF16)
    row = lambda v: v.reshape(depth, 1, v.shape[-1])
    wglu, wa, wb, wc, wo = bf(ssm_w_glu), bf(w_branch_a), bf(w_branch_b), bf(w_branch_c), bf(w_out)
    bglu, dskip, lng, lnb = row(ssm_b_glu), row(ssm_d), row(ln_g), row(ln_b)
    gq, gkv = row(mla_q_norm), row(mla_kv_norm)

    x_lat, x_ctx, x_ctx_tile0 = x, ctx, 0
    for l in range(depth):
        n_out_tiles = n_lat_tiles if l == depth - 1 else ltot // TOKEN_TILE
        u, z, na, mc, g = _in_call(x_lat, x_ctx, x_ctx_tile0, ltot, modt, w_pad, n_lat_tiles, l)
        y = _s5_call(u, s5_wc, s5_tc, s5_mc, s5_a, rep_state, rep_out, nlat, nctx, l)
        yn_lat, yn_ctx = _na_call(na, bias_all, nlat, nctx, l), _na_ctx_call(na, nlat, nctx)
        q, k, vt = _mlap_call(mc, cos_t, sin_t, gq, gkv, wq, wqs, wk, wks, wv, ones, l)
        ym_lat = _fa_call(q, k, vt, nlat, 0, ltot, 0, fa_tq, fa_tk)
        ym_ctx = _fa_call(q, k, vt, nctx, nlat, nctx, nlat, nctx, nctx)
        xcat = _merge_call(y, u, z, g, yn_lat, yn_ctx, ym_lat, ym_ctx, x_lat, x_ctx, x_ctx_tile0, modt,
                           wglu, bglu, dskip, wa, wb, wc, wo, lng, lnb, n_lat_tiles, alpha, l, n_out_tiles)
        x_lat, x_ctx, x_ctx_tile0 = xcat, xcat, n_lat_tiles
    return xcat
```
